```python
import jax
import jax.numpy as jnp
from jax import lax
import numpy as np

D_MODEL = 1024
BATCH = 4
SEQ = 8192
DEPTH = 4

N_MIXERS = 4
MEM_LEN = 256
NORM_EPS = 1e-6
ROPE_THETA = 500000.0
NEG_INF = -1e30
POS_OFFSET_MAX = 4096

A_HEAD_DIM = 64
A_HEADS_PER_GROUP = 8
A_GROUPS = ((128, 1), (512, 4), (2048, 16))
A_BLOCK = 128
A_ROT_DIM = A_HEAD_DIM // 4

B_HEADS = 4
B_KEY_DIM = D_MODEL // 2
B_VAL_DIM = D_MODEL
B_DK = B_KEY_DIM // B_HEADS
B_DV = B_VAL_DIM // B_HEADS
B_GATE_RANK = 16
B_GATE_NORMALIZER = 16.0
B_CHUNK = 64

C_HEADS = 16
C_Q_RANK = 384
C_KV_RANK = 256
C_NOPE = 64
C_ROPE = 32
C_VDIM = 64
C_QBLOCK = 128

D_HEAD = 64
D_HEADS = D_MODEL // D_HEAD
D_DECAY_RANK = 64
D_ICL_RANK = 64
D_GATE_RANK = 128
D_GN_EPS = 64e-5

M_HEADS = 4
M_HEAD_DIM = 128

D_FF = 2816
CONV_WIDTH = 3

kernel_name = 'hybrid_interleaved_dilated_gla_mla_rwkv7_trunk'


def _n_occ(m):
    return len(range(m, DEPTH, N_MIXERS))


def rms_norm(x, g, eps=NORM_EPS):
    xf = x.astype(jnp.float32)
    y = xf * lax.rsqrt(jnp.mean(xf * xf, axis=-1, keepdims=True) + eps)
    return (y * g.astype(jnp.float32)).astype(x.dtype)


def apply_rope(x, pos, rot_dim):
    half = rot_dim // 2
    inv_freq = ROPE_THETA ** (-(jnp.arange(half, dtype=jnp.float32) * (2.0 / rot_dim)))
    ang = pos.astype(jnp.float32)[:, :, None] * inv_freq
    cos = jnp.cos(ang)[:, :, None, :]
    sin = jnp.sin(ang)[:, :, None, :]
    x1 = x[..., :half].astype(jnp.float32)
    x2 = x[..., half:rot_dim].astype(jnp.float32)
    rot = jnp.concatenate([x1 * cos - x2 * sin, x2 * cos + x1 * sin], axis=-1).astype(x.dtype)
    return jnp.concatenate([rot, x[..., rot_dim:]], axis=-1)


def _to_blocks(t, dil):
    B, S, H, d = t.shape
    L = S // dil
    nb = -(-L // A_BLOCK)
    t = t.reshape(B, L, dil, H, d).transpose(0, 2, 1, 3, 4)
    t = jnp.pad(t, ((0, 0), (0, 0), (0, nb * A_BLOCK - L), (0, 0), (0, 0)))
    return t.reshape(B, dil, nb, A_BLOCK, H, d)


def _from_blocks(t, S):
    B, dil = t.shape[:2]
    tail = t.shape[4:]
    t = t.reshape((B, dil, -1) + tail)[:, :, : S // dil]
    return jnp.moveaxis(t, 1, 2).reshape((B, S) + tail)


def dilated_window_attention(q, k, v, dil, steps):
    S = q.shape[1]
    qb, kb, vb = _to_blocks(q, dil), _to_blocks(k, dil), _to_blocks(v, dil)
    nb = qb.shape[2]
    pad_prev = ((0, 0), (0, 0), (1, 0), (0, 0), (0, 0), (0, 0))
    kk = jnp.concatenate([jnp.pad(kb, pad_prev)[:, :, :-1], kb], axis=3)
    vv = jnp.concatenate([jnp.pad(vb, pad_prev)[:, :, :-1], vb], axis=3)
    s = jnp.einsum('brnqhd,brnkhd->brnhqk', qb, kk).astype(jnp.float32)
    qi = np.arange(A_BLOCK)[:, None]
    kj = np.arange(2 * A_BLOCK)[None, :]
    dist = A_BLOCK + qi - kj
    band = (dist >= 0) & (dist <= steps)
    mask = np.broadcast_to(band, (nb, A_BLOCK, 2 * A_BLOCK)).copy()
    mask[0] &= kj >= A_BLOCK
    s = jnp.where(jnp.asarray(mask)[None, None, :, None], s, NEG_INF)
    m = jnp.max(s, axis=-1, keepdims=True)
    p = jnp.exp(s - m)
    l = jnp.sum(p, axis=-1, keepdims=True)
    o = jnp.einsum('brnhqk,brnkhd->brnhqd', p, vv.astype(jnp.float32)) / l
    lse = (m + jnp.log(l))[..., 0]
    o = _from_blocks(o.transpose(0, 1, 2, 4, 3, 5), S)
    lse = _from_blocks(lse.transpose(0, 1, 2, 4, 3), S)
    return o, lse


def mixer_dilated(h, pos, w_qkv, w_o):
    B, S, _ = h.shape
    G = len(A_GROUPS)
    qkv = (h @ w_qkv).reshape(B, S, 3, G, A_HEADS_PER_GROUP, A_HEAD_DIM)
    outs, lses = [], []
    for g, (window, dil) in enumerate(A_GROUPS):
        q = apply_rope(qkv[:, :, 0, g], pos, A_ROT_DIM) * (A_HEAD_DIM ** -0.5)
        k = apply_rope(qkv[:, :, 1, g], pos, A_ROT_DIM)
        o, lse = dilated_window_attention(q, k, qkv[:, :, 2, g], dil, window // dil)
        outs.append(o)
        lses.append(lse)
    alpha = jax.nn.softmax(jnp.stack(lses), axis=0)[..., None]
    o = jnp.sum(alpha * jnp.stack(outs), axis=0).astype(h.dtype)
    return o.reshape(B, S, A_HEADS_PER_GROUP * A_HEAD_DIM) @ w_o


def mixer_gla(h, w_in, w_gate2, b_gate, o_norm, w_o):
    B, S, _ = h.shape
    H, dk, dv, C = B_HEADS, B_DK, B_DV, B_CHUNK
    n = S // C
    f32 = jnp.float32
    cuts = np.cumsum([B_KEY_DIM, B_KEY_DIM, B_VAL_DIM, B_VAL_DIM]).tolist()
    q, k, v, gate_out, gate_lr = jnp.split(h @ w_in, cuts, axis=-1)
    log_a = jax.nn.log_sigmoid((gate_lr @ w_gate2 + b_gate).astype(f32)) / B_GATE_NORMALIZER

    def chunks(t, d):
        return t.reshape(B, n, C, H, d)

    qc = chunks(q, dk).astype(f32) * (dk ** -0.5)
    kc = chunks(k, dk).astype(f32)
    vc = chunks(v, dv).astype(f32)
    G = jnp.cumsum(chunks(log_a, dk), axis=2)
    g_end = G[:, :, -1]
    q_dec = qc * jnp.exp(G)
    k_inv = kc * jnp.exp(-G)
    k_end = kc * jnp.exp(g_end[:, :, None] - G)
    causal = jnp.asarray(np.tril(np.ones((C, C), dtype=bool)))
    att = jnp.where(causal, jnp.einsum('bnihd,bnjhd->bnhij', q_dec, k_inv), 0.0)
    o_intra = jnp.einsum('bnhij,bnjhe->bnihe', att, vc)

    def step(state, inp):
        q_t, k_t, v_t, ge_t = inp
        o_t = jnp.einsum('bihd,bhde->bihe', q_t, state)
        state = state * jnp.exp(ge_t)[..., None] + jnp.einsum('bjhd,bjhe->bhde', k_t, v_t)
        return state, o_t

    xs = tuple(jnp.moveaxis(t, 1, 0) for t in (q_dec, k_end, vc, g_end))
    _, o_inter = lax.scan(step, jnp.zeros((B, H, dk, dv), f32), xs)
    o = (o_intra + jnp.moveaxis(o_inter, 0, 1)).reshape(B, S, H, dv)
    o = rms_norm(o, o_norm) * jax.nn.silu(gate_out.astype(f32).reshape(B, S, H, dv))
    return o.reshape(B, S, B_VAL_DIM).astype(h.dtype) @ w_o


def mixer_mla(h, pos, w_in, q_norm, w_uq, kv_norm, w_ukv, w_o):
    B, S, _ = h.shape
    H = C_HEADS
    cq, ckv, k_pe = jnp.split(h @ w_in, [C_Q_RANK, C_Q_RANK + C_KV_RANK], axis=-1)
    q = (rms_norm(cq, q_norm) @ w_uq).reshape(B, S, H, C_NOPE + C_ROPE)
    q_nope = q[..., :C_NOPE]
    q_pe = apply_rope(q[..., C_NOPE:], pos, C_ROPE)
    kv = (rms_norm(ckv, kv_norm) @ w_ukv).reshape(B, S, H, C_NOPE + C_VDIM)
    k_nope, v = kv[..., :C_NOPE], kv[..., C_NOPE:]
    k_pe = apply_rope(k_pe[:, :, None, :], pos, C_ROPE)[:, :, 0]
    scale = (C_NOPE + C_ROPE) ** -0.5
    nb = S // C_QBLOCK
    key_pos = jnp.arange(S)

    def blockify(t):
        return jnp.moveaxis(t.reshape((B, nb, C_QBLOCK) + t.shape[2:]), 1, 0)

    def attend(args):
        qn, qp, b = args
        s = (jnp.einsum('bqhd,bkhd->bhqk', qn, k_nope)
             + jnp.einsum('bqhr,bkr->bhqk', qp, k_pe)).astype(jnp.float32) * scale
        q_pos = b * C_QBLOCK + jnp.arange(C_QBLOCK)
        s = jnp.where(key_pos[None, :] <= q_pos[:, None], s, NEG_INF)
        p = jax.nn.softmax(s, axis=-1).astype(v.dtype)
        return jnp.einsum('bhqk,bkhd->bqhd', p, v)

    o = lax.map(attend, (blockify(q_nope), blockify(q_pe), jnp.arange(nb)))
    o = jnp.moveaxis(o, 0, 1).reshape(B, S, H * C_VDIM)
    return o @ w_o


def mixer_rwkv7(h, mix, w_rkv, w0, w1, w2, a0, a1, a2, g1, g2, k_k, k_a, r_k, lnx_w, lnx_b, w_o):
    B, S, D = h.shape
    H, N = D_HEADS, D_HEAD
    f32 = jnp.float32
    xx = jnp.pad(h, ((0, 0), (1, 0), (0, 0)))[:, :-1] - h
    xr, xw, xk, xv, xa, xg = (h + xx * mix[i] for i in range(6))
    r = xr @ w_rkv[0]
    k = xk @ w_rkv[1]
    v = xv @ w_rkv[2]
    w = -jax.nn.softplus(-(w0 + jnp.tanh(xw @ w1) @ w2).astype(f32)) - 0.5
    a = jax.nn.sigmoid((a0 + (xa @ a1) @ a2).astype(f32))
    g = jax.nn.sigmoid(xg @ g1) @ g2
    kk = (k * k_k).astype(f32).reshape(B, S, H, N)
    kk = kk / jnp.maximum(jnp.linalg.norm(kk, axis=-1, keepdims=True), 1e-12)
    k = k.astype(f32) * (1.0 + (a - 1.0) * k_a)

    def heads(t):
        return t.astype(f32).reshape(B, S, H, N)

    r_h, k_h, v_h, a_h = heads(r), heads(k), heads(v), heads(a)
    decay = heads(jnp.exp(-jnp.exp(w)))

    def step(state, inp):
        r_t, w_t, k_t, v_t, kk_t, a_t = inp
        sa = jnp.einsum('bhvk,bhk->bhv', state, -kk_t)
        state = (state * w_t[:, :, None, :]
                 + sa[..., None] * (kk_t * a_t)[:, :, None, :]
                 + v_t[..., None] * k_t[:, :, None, :])
        return state, jnp.einsum('bhvk,bhk->bhv', state, r_t)

    xs = tuple(jnp.moveaxis(t, 1, 0) for t in (r_h, decay, k_h, v_h, kk, a_h))
    _, y = lax.scan(step, jnp.zeros((B, H, N, N), f32), xs)
    y = jnp.moveaxis(y, 0, 1)
    mu = jnp.mean(y, axis=-1, keepdims=True)
    var = jnp.mean(jnp.square(y - mu), axis=-1, keepdims=True)
    y = ((y - mu) * lax.rsqrt(var + D_GN_EPS)).reshape(B, S, D) * lnx_w + lnx_b
    bonus = jnp.sum(r_h * k_h * r_k, axis=-1, keepdims=True) * v_h
    y = y + bonus.reshape(B, S, D)
    return (y * g).astype(h.dtype) @ w_o


def memory_attention(h, mem_k, mem_v, w_q, w_o):
    B, S, _ = h.shape
    q = (h @ w_q).reshape(B, S, M_HEADS, M_HEAD_DIM)
    s = jnp.einsum('bshd,bmhd->bhsm', q, mem_k).astype(jnp.float32) * (M_HEAD_DIM ** -0.5)
    p = jax.nn.softmax(s, axis=-1).astype(mem_v.dtype)
    o = jnp.einsum('bhsm,bmhd->bshd', p, mem_v)
    return o.reshape(B, S, M_HEADS * M_HEAD_DIM) @ w_o


def conv_ffn(h, w_in, conv_w, conv_b, w_out):
    S = h.shape[1]
    u = h @ w_in
    up = jnp.pad(u, ((0, 0), (CONV_WIDTH - 1, 0), (0, 0)))
    c = conv_b + up[:, :S] * conv_w[0]
    for j in range(1, CONV_WIDTH):
        c = c + up[:, j:j + S] * conv_w[j]
    gate, val = jnp.split(c, 2, axis=-1)
    return (jax.nn.silu(gate) * val) @ w_out


def setup_inputs(seed: int = 0) -> dict:
    key = jax.random.key(seed)
    keys = iter(jax.random.split(key, 64))
    f32 = jnp.float32
    D = D_MODEL
    nA, nB, nC, nD = (_n_occ(m) for m in range(N_MIXERS))

    def dense(shape, fan_in, scale=1.0):
        return jax.random.normal(next(keys), shape, f32) * (scale * fan_in ** -0.5)

    def gain(shape):
        return 1.0 + 0.05 * jax.random.normal(next(keys), shape, f32)

    def small(shape, scale=0.02):
        return scale * jax.random.normal(next(keys), shape, f32)

    def uniform(shape, lo, hi):
        return jax.random.uniform(next(keys), shape, f32, lo, hi)

    a_width = 3 * len(A_GROUPS) * A_HEADS_PER_GROUP * A_HEAD_DIM
    b_width = 2 * B_KEY_DIM + 2 * B_VAL_DIM + B_GATE_RANK
    c_width = C_Q_RANK + C_KV_RANK + C_ROPE
    return {
        'x': jax.random.normal(next(keys), (BATCH, SEQ, D), f32),
        'mem': jax.random.normal(next(keys), (BATCH, MEM_LEN, D), f32),
        'positions': (jax.random.randint(next(keys), (BATCH, 1), 0, POS_OFFSET_MAX, jnp.int32)
                      + jnp.arange(SEQ, dtype=jnp.int32)[None, :]),
        'ln_gains': gain((DEPTH, 6, D)),
        'mem_norm': gain((D,)),
        'mem_w_kv': dense((D, 2 * M_HEADS * M_HEAD_DIM), D),
        'mem_w_q': dense((DEPTH, D, M_HEADS * M_HEAD_DIM), D),
        'mem_w_o': dense((DEPTH, M_HEADS * M_HEAD_DIM, D), M_HEADS * M_HEAD_DIM),
        'ffn_w_in': dense((DEPTH, D, 2 * D_FF), D),
        'ffn_conv_w': dense((DEPTH, CONV_WIDTH, 2 * D_FF), CONV_WIDTH),
        'ffn_conv_b': small((DEPTH, 2 * D_FF)),
        'ffn_w_out': dense((DEPTH, D_FF, D), D_FF),
        'a_w_qkv': dense((nA, D, a_width), D),
        'a_w_o': dense((nA, A_HEADS_PER_GROUP * A_HEAD_DIM, D), A_HEADS_PER_GROUP * A_HEAD_DIM),
        'b_w_in': dense((nB, D, b_width), D),
        'b_w_gate2': dense((nB, B_GATE_RANK, B_KEY_DIM), B_GATE_RANK),
        'b_gate_bias': small((nB, B_KEY_DIM), 0.1),
        'b_o_norm': gain((nB, B_DV)),
        'b_w_o': dense((nB, B_VAL_DIM, D), B_VAL_DIM),
        'c_w_in': dense((nC, D, c_width), D),
        'c_q_norm': gain((nC, C_Q_RANK)),
        'c_w_uq': dense((nC, C_Q_RANK, C_HEADS * (C_NOPE + C_ROPE)), C_Q_RANK),
        'c_kv_norm': gain((nC, C_KV_RANK)),
        'c_w_ukv': dense((nC, C_KV_RANK, C_HEADS * (C_NOPE + C_VDIM)), C_KV_RANK),
        'c_w_o': dense((nC, C_HEADS * C_VDIM, D), C_HEADS * C_VDIM),
        'd_mix': uniform((nD, 6, D), 0.0, 1.0),
        'd_w_rkv': dense((nD, 3, D, D), D),
        'd_w0': uniform((nD, D), -6.0, -1.0),
        'd_w1': dense((nD, D, D_DECAY_RANK), D),
        'd_w2': dense((nD, D_DECAY_RANK, D), D_DECAY_RANK, 0.1),
        'd_a0': small((nD, D), 0.1),
        'd_a1': dense((nD, D, D_ICL_RANK), D),
        'd_a2': dense((nD, D_ICL_RANK, D), D_ICL_RANK),
        'd_g1': dense((nD, D, D_GATE_RANK), D),
        'd_g2': dense((nD, D_GATE_RANK, D), D_GATE_RANK),
        'd_k_k': 0.85 + small((nD, D), 0.05),
        'd_k_a': gain((nD, D)),
        'd_r_k': small((nD, D_HEADS, D_HEAD), 0.1),
        'd_lnx_w': gain((nD, D)),
        'd_lnx_b': small((nD, D)),
        'd_w_o': dense((nD, D, D), D),
    }


def reference(x, mem, positions, ln_gains, mem_norm, mem_w_kv, mem_w_q, mem_w_o,
              ffn_w_in, ffn_conv_w, ffn_conv_b, ffn_w_out,
              a_w_qkv, a_w_o,
              b_w_in, b_w_gate2, b_gate_bias, b_o_norm, b_w_o,
              c_w_in, c_q_norm, c_w_uq, c_kv_norm, c_w_ukv, c_w_o,
              d_mix, d_w_rkv, d_w0, d_w1, d_w2, d_a0, d_a1, d_a2, d_g1, d_g2,
              d_k_k, d_k_a, d_r_k, d_lnx_w, d_lnx_b, d_w_o):
    B, L_mem, _ = mem.shape
    mkv = (rms_norm(mem, mem_norm) @ mem_w_kv).reshape(B, L_mem, 2, M_HEADS, M_HEAD_DIM)
    mem_k, mem_v = mkv[:, :, 0], mkv[:, :, 1]
    for i in range(DEPTH):
        m, j = i % N_MIXERS, i // N_MIXERS
        hn = rms_norm(x, ln_gains[i, 0])
        if m == 0:
            y = mixer_dilated(hn, positions, a_w_qkv[j], a_w_o[j])
        elif m == 1:
            y = mixer_gla(hn, b_w_in[j], b_w_gate2[j], b_gate_bias[j], b_o_norm[j], b_w_o[j])
        elif m == 2:
            y = mixer_mla(hn, positions, c_w_in[j], c_q_norm[j], c_w_uq[j],
                          c_kv_norm[j], c_w_ukv[j], c_w_o[j])
        else:
            y = mixer_rwkv7(hn, d_mix[j], d_w_rkv[j], d_w0[j], d_w1[j], d_w2[j],
                            d_a0[j], d_a1[j], d_a2[j], d_g1[j], d_g2[j],
                            d_k_k[j], d_k_a[j], d_r_k[j], d_lnx_w[j], d_lnx_b[j], d_w_o[j])
        x = x + rms_norm(y, ln_gains[i, 1])
        y = memory_attention(rms_norm(x, ln_gains[i, 2]), mem_k, mem_v, mem_w_q[i], mem_w_o[i])
        x = x + rms_norm(y, ln_gains[i, 3])
        y = conv_ffn(rms_norm(x, ln_gains[i, 4]), ffn_w_in[i], ffn_conv_w[i], ffn_conv_b[i], ffn_w_out[i])
        x = x + rms_norm(y, ln_gains[i, 5])
    return x
```

```python
import functools
import math

import jax
import jax.numpy as jnp
from jax import lax
from jax.experimental import pallas as pl
from jax.experimental.pallas import tpu as pltpu

F32 = jnp.float32
BF16 = jnp.bfloat16
HIGHEST = lax.Precision.HIGHEST

D_MODEL = 1024
N_MIXERS = 4
NORM_EPS = 1e-6
ROPE_THETA = 500000.0
NEG_INF = -1e30

A_HEAD_DIM = 64
A_HEADS = 8
A_GROUPS = ((128, 1), (512, 4), (2048, 16))
A_BLOCK = 128
A_ROT = A_HEAD_DIM // 4
A_WIDTH = A_HEADS * A_HEAD_DIM

B_HEADS = 4
B_KEY_DIM = 512
B_VAL_DIM = 1024
B_DK = B_KEY_DIM // B_HEADS
B_DV = B_VAL_DIM // B_HEADS
B_GATE_RANK = 16
B_GATE_NORMALIZER = 16.0
B_CHUNK = 64

C_HEADS = 16
C_Q_RANK = 384
C_KV_RANK = 256
C_NOPE = 64
C_ROPE = 32
C_VDIM = 64
C_SLOT = 128

D_HEAD = 64
D_HEADS = D_MODEL // D_HEAD
D_GN_EPS = 64e-5
D_CHUNK = 64
D_SUB = 16

M_HEADS = 4
M_HEAD_DIM = 128

D_FF = 2816
CONV_WIDTH = 3
FF_CHUNK = 256

LANES = 128
SUBLANES = 8
VMEM_LIMIT = 56 * 1024 * 1024


def _cparams(*sem):
    return pltpu.CompilerParams(dimension_semantics=sem, vmem_limit_bytes=VMEM_LIMIT)


def _rms(x, eps=NORM_EPS):
    return x * lax.rsqrt(jnp.mean(x * x, axis=-1, keepdims=True) + eps)


def _sigmoid(x):
    return 1.0 / (1.0 + jnp.exp(-x))


def _softplus(x):
    return jnp.maximum(x, 0.0) + jnp.log(1.0 + jnp.exp(-jnp.abs(x)))


def _dot(a, b, **kw):
    return jnp.dot(a, b, preferred_element_type=F32, **kw)


def _dot_t(a, b, **kw):
    return lax.dot_general(a, b, (((1,), (1,)), ((), ())), preferred_element_type=F32, **kw)


def _tdot(a, b, **kw):
    return lax.dot_general(a, b, (((0,), (0,)), ((), ())), preferred_element_type=F32, **kw)


def _rope_tile(y, c, s1, s2, shift):
    return y * c + pltpu.roll(y, LANES - shift, 1) * s1 + pltpu.roll(y, shift, 1) * s2


def _rope_table_kernel(pos_ref, ca_ref, sa1_ref, sa2_ref, cc_ref, sc1_ref, sc2_ref):
    pos = pos_ref[...].astype(F32)
    lane = lax.broadcasted_iota(jnp.int32, (1, LANES), 1)
    log_theta = math.log(ROPE_THETA)

    def tables(rel, half, rot):
        in_span = (rel >= 0) & (rel < rot)
        idx = jnp.where(in_span, jnp.where(rel < half, rel, rel - half), 0).astype(F32)
        inv_freq = jnp.exp(-(idx * (2.0 / rot)) * log_theta)
        ang = pos * inv_freq
        cos, sin = jnp.cos(ang), jnp.sin(ang)
        first = (rel >= 0) & (rel < half)
        second = (rel >= half) & (rel < rot)
        return (jnp.where(in_span, cos, 1.0), jnp.where(first, -sin, 0.0), jnp.where(second, sin, 0.0))

    ca, sa1, sa2 = tables(lane % A_HEAD_DIM, A_ROT // 2, A_ROT)
    cc, sc1, sc2 = tables(lane - C_NOPE, C_ROPE // 2, C_ROPE)
    ca_ref[...], sa1_ref[...], sa2_ref[...] = ca, sa1, sa2
    cc_ref[...], sc1_ref[...], sc2_ref[...] = cc, sc1, sc2


def _rope_tables(positions):
    t = positions.size
    tm = min(t, 1024)
    pos = positions.reshape(t, 1)
    out = jax.ShapeDtypeStruct((t, LANES), F32)
    spec = pl.BlockSpec((tm, LANES), lambda i: (i, 0))
    return pl.pallas_call(
        _rope_table_kernel,
        grid=(t // tm,),
        in_specs=[pl.BlockSpec((tm, 1), lambda i: (i, 0))],
        out_specs=[spec] * 6,
        out_shape=[out] * 6,
        compiler_params=_cparams("parallel"),
        name="rope_tables",
    )(pos)


def _norm_matmul_kernel(*refs, rope_blocks, scale_blocks, scale, shift):
    if rope_blocks:
        x_ref, g_ref, w_ref, c_ref, s1_ref, s2_ref, o_ref, xn_ref = refs
    else:
        x_ref, g_ref, w_ref, o_ref, xn_ref = refs
    j = pl.program_id(1)

    @pl.when(j == 0)
    def _():
        xn_ref[...] = (_rms(x_ref[...]) * g_ref[...]).astype(BF16)

    y = _dot(xn_ref[...], w_ref[...])
    if rope_blocks:
        on = j < rope_blocks
        c = jnp.where(on, c_ref[...], 1.0)
        s1 = jnp.where(on, s1_ref[...], 0.0)
        s2 = jnp.where(on, s2_ref[...], 0.0)
        sc = jnp.where(j < scale_blocks, scale, 1.0)
        tn = y.shape[1]
        y = jnp.concatenate(
            [_rope_tile(y[:, s * LANES:(s + 1) * LANES], c, s1, s2, shift) for s in range(tn // LANES)], axis=1) * sc
    o_ref[...] = y.astype(o_ref.dtype)


def _norm_matmul(x, g, w, *, out_dtype, tn, tm=1024, k=None, k_block=0, rope=None):
    t = x.shape[0]
    k = x.shape[1] if k is None else k
    n = w.shape[1]
    tm = min(tm, t)
    in_specs = [
        pl.BlockSpec((tm, k), lambda i, j: (i, k_block)),
        pl.BlockSpec((1, k), lambda i, j: (0, 0)),
        pl.BlockSpec((k, tn), lambda i, j: (0, j)),
    ]
    args = [x, g.reshape(1, k), w]
    kw = dict(rope_blocks=0, scale_blocks=0, scale=1.0, shift=0)
    if rope is not None:
        tabs, kw = rope
        in_specs += [pl.BlockSpec((tm, LANES), lambda i, j: (i, 0))] * 3
        args += list(tabs)
    return pl.pallas_call(
        functools.partial(_norm_matmul_kernel, **kw),
        grid=(t // tm, n // tn),
        in_specs=in_specs,
        out_specs=pl.BlockSpec((tm, tn), lambda i, j: (i, j)),
        out_shape=jax.ShapeDtypeStruct((t, n), out_dtype),
        scratch_shapes=[pltpu.VMEM((tm, k), BF16)],
        compiler_params=_cparams("parallel", "arbitrary"),
        name="norm_matmul",
    )(*args)


def _matmul_norm_res_kernel(*refs, gated):
    if gated:
        h_ref, gate_ref, w_ref, g_ref, res_ref, o_ref = refs
        h = (h_ref[...] * gate_ref[...].astype(F32)).astype(BF16)
    else:
        h_ref, w_ref, g_ref, res_ref, o_ref = refs
        h = h_ref[...]
    y = _dot(h, w_ref[...])
    o_ref[...] = res_ref[...] + _rms(y) * g_ref[...]


def _matmul_norm_res(h, w, g, res, gate=None, tm=512):
    t, k = h.shape
    n = w.shape[1]
    tm = min(tm, t)
    row = lambda width: pl.BlockSpec((tm, width), lambda i: (i, 0))
    full = lambda a: pl.BlockSpec(a.shape, lambda i: (0, 0))
    g2 = g.reshape(1, n)
    if gate is None:
        args, specs = [h, w, g2, res], [row(k), full(w), full(g2), row(n)]
    else:
        args, specs = [h, gate, w, g2, res], [row(k), row(k), full(w), full(g2), row(n)]
    return pl.pallas_call(
        functools.partial(_matmul_norm_res_kernel, gated=gate is not None),
        grid=(t // tm,),
        in_specs=specs,
        out_specs=row(n),
        out_shape=jax.ShapeDtypeStruct((t, n), F32),
        compiler_params=_cparams("parallel"),
        name="matmul_norm_res",
    )(*args)


def _mem_attn_kernel(x_ref, gpre_ref, wq_ref, k_ref, v_ref, wo_ref, gpost_ref, o_ref):
    x = x_ref[...]
    hn = (_rms(x) * gpre_ref[...]).astype(BF16)
    q = (_dot(hn, wq_ref[...]) * (M_HEAD_DIM ** -0.5)).astype(BF16)
    k, v = k_ref[...], v_ref[...]
    outs = []
    for h in range(M_HEADS):
        sl = slice(h * M_HEAD_DIM, (h + 1) * M_HEAD_DIM)
        s = _dot_t(q[:, sl], k[:, sl])
        p = jnp.exp(s - jnp.max(s, axis=-1, keepdims=True))
        p = p / jnp.sum(p, axis=-1, keepdims=True)
        outs.append(_dot(p.astype(BF16), v[:, sl]))
    o = jnp.concatenate(outs, axis=1).astype(BF16)
    y = _dot(o, wo_ref[...])
    o_ref[...] = x + _rms(y) * gpost_ref[...]


def _mem_attn(x, gpre, wq, mem_k, mem_v, wo, gpost, tm=512):
    b, s, d = x.shape
    tm = min(tm, s)
    width = M_HEADS * M_HEAD_DIM
    mlen = mem_k.shape[1]
    const = lambda a: pl.BlockSpec(a.shape, lambda bi, i: (0,) * a.ndim)
    gpre, gpost = gpre.reshape(1, d), gpost.reshape(1, d)
    xspec = pl.BlockSpec((None, tm, d), lambda bi, i: (bi, i, 0))
    mspec = pl.BlockSpec((None, mlen, width), lambda bi, i: (bi, 0, 0))
    return pl.pallas_call(
        _mem_attn_kernel,
        grid=(b, s // tm),
        in_specs=[xspec, const(gpre), const(wq), mspec, mspec, const(wo), const(gpost)],
        out_specs=xspec,
        out_shape=jax.ShapeDtypeStruct(x.shape, F32),
        compiler_params=_cparams("parallel", "parallel"),
        name="mem_attn",
    )(x, gpre, wq, mem_k, mem_v, wo, gpost)


def _conv_ffn_kernel(x_ref, gpre_ref, win_ref, cw_ref, cb_ref, wout_ref, gpost_ref, o_ref,
                     hn_ref, acc_ref, ubuf_ref, carry_ref):
    i, c = pl.program_id(1), pl.program_id(2)
    tm = x_ref.shape[0]
    fc = wout_ref.shape[0]

    @pl.when(c == 0)
    def _():
        hn_ref[...] = (_rms(x_ref[...]) * gpre_ref[...]).astype(BF16)
        acc_ref[...] = jnp.zeros_like(acc_ref)

    u = _dot(hn_ref[...], win_ref[...])
    @pl.when(i == 0)
    def _():
        ubuf_ref[0:SUBLANES, :] = jnp.zeros((SUBLANES, 2 * fc), F32)

    @pl.when(i > 0)
    def _():
        ubuf_ref[0:SUBLANES, :] = carry_ref[c]

    ubuf_ref[SUBLANES:SUBLANES + tm, :] = u
    carry_ref[c] = u[tm - SUBLANES:, :]
    cw = cw_ref[...]
    conv = (cb_ref[...] + ubuf_ref[SUBLANES - 2:SUBLANES - 2 + tm, :] * cw[0:1]
            + ubuf_ref[SUBLANES - 1:SUBLANES - 1 + tm, :] * cw[1:2] + u * cw[2:3])
    gate, val = conv[:, :fc], conv[:, fc:]
    h = (gate * _sigmoid(gate) * val).astype(BF16)
    acc_ref[...] += _dot(h, wout_ref[...])

    @pl.when(c == pl.num_programs(2) - 1)
    def _():
        o_ref[...] = x_ref[...] + _rms(acc_ref[...]) * gpost_ref[...]


def _conv_ffn(x, gpre, w_in, conv_w, conv_b, w_out, gpost, tm=1024):
    b, s, d = x.shape
    tm = min(tm, s)
    fc = FF_CHUNK
    nc = D_FF // fc

    def interleave(a):
        lead = a.shape[:-1]
        a = a.reshape(lead + (2, nc, fc))
        return jnp.swapaxes(a, -3, -2).reshape(lead + (2 * D_FF,))

    w_in = interleave(w_in).astype(BF16)
    conv_w = interleave(conv_w)
    conv_b = interleave(conv_b).reshape(1, 2 * D_FF)
    w_out = w_out.astype(BF16)
    gpre, gpost = gpre.reshape(1, d), gpost.reshape(1, d)
    xspec = pl.BlockSpec((None, tm, d), lambda bi, i, c: (bi, i, 0))
    gspec = pl.BlockSpec((1, d), lambda bi, i, c: (0, 0))
    return pl.pallas_call(
        _conv_ffn_kernel,
        grid=(b, s // tm, nc),
        in_specs=[
            xspec, gspec,
            pl.BlockSpec((d, 2 * fc), lambda bi, i, c: (0, c)),
            pl.BlockSpec((CONV_WIDTH, 2 * fc), lambda bi, i, c: (0, c)),
            pl.BlockSpec((1, 2 * fc), lambda bi, i, c: (0, c)),
            pl.BlockSpec((fc, d), lambda bi, i, c: (c, 0)),
            gspec,
        ],
        out_specs=xspec,
        out_shape=jax.ShapeDtypeStruct(x.shape, F32),
        scratch_shapes=[
            pltpu.VMEM((tm, d), BF16),
            pltpu.VMEM((tm, d), F32),
            pltpu.VMEM((tm + SUBLANES, 2 * fc), F32),
            pltpu.VMEM((nc, SUBLANES, 2 * fc), F32),
        ],
        compiler_params=_cparams("arbitrary", "arbitrary", "arbitrary"),
        name="conv_ffn",
    )(x, gpre, w_in, conv_w, conv_b, w_out, gpost)


def _dilated_attn_kernel(q_ref, kc_ref, kp_ref, vc_ref, vp_ref, o_ref, lse_ref):
    n = pl.program_id(2)
    q, kc, kp, vc, vp = q_ref[...], kc_ref[...], kp_ref[...], vc_ref[...], vp_ref[...]
    qi = lax.broadcasted_iota(jnp.int32, (A_BLOCK, A_BLOCK), 0)
    kj = lax.broadcasted_iota(jnp.int32, (A_BLOCK, A_BLOCK), 1)
    mask_p = (kj >= qi) & (n > 0)
    mask_c = kj <= qi
    outs, lses = [], []
    for h in range(A_HEADS):
        sl = slice(h * A_HEAD_DIM, (h + 1) * A_HEAD_DIM)
        s_p = jnp.where(mask_p, _dot_t(q[:, sl], kp[:, sl]), NEG_INF)
        s_c = jnp.where(mask_c, _dot_t(q[:, sl], kc[:, sl]), NEG_INF)
        m = jnp.maximum(jnp.max(s_p, axis=-1, keepdims=True), jnp.max(s_c, axis=-1, keepdims=True))
        p_p, p_c = jnp.exp(s_p - m), jnp.exp(s_c - m)
        l = jnp.sum(p_p, axis=-1, keepdims=True) + jnp.sum(p_c, axis=-1, keepdims=True)
        o = (_dot(p_p.astype(BF16), vp[:, sl]) + _dot(p_c.astype(BF16), vc[:, sl])) / l
        outs.append(o)
        lses.append(jnp.broadcast_to(m + jnp.log(l), (A_BLOCK, A_HEAD_DIM)))
    o_ref[...] = jnp.concatenate(outs, axis=1).astype(o_ref.dtype)
    lse_ref[...] = jnp.concatenate(lses, axis=1)


def _dilated_attn(qkv, g, dil):
    b, s, width = qkv.shape
    sections = width // A_WIDTH
    ln = s // dil
    nb = ln // A_BLOCK
    x = qkv.reshape(b, ln, dil * width)

    def spec(section, prev):
        def index(bi, r, n):
            return (bi, jnp.maximum(n - 1, 0) if prev else n, r * sections + section * len(A_GROUPS) + g)
        return pl.BlockSpec((None, A_BLOCK, A_WIDTH), index)

    ospec = pl.BlockSpec((None, A_BLOCK, A_WIDTH), lambda bi, r, n: (bi, n, r))
    o, lse = pl.pallas_call(
        _dilated_attn_kernel,
        grid=(b, dil, nb),
        in_specs=[spec(0, False), spec(1, False), spec(1, True), spec(2, False), spec(2, True)],
        out_specs=[ospec, ospec],
        out_shape=[jax.ShapeDtypeStruct((b, ln, dil * A_WIDTH), BF16),
                   jax.ShapeDtypeStruct((b, ln, dil * A_WIDTH), F32)],
        compiler_params=_cparams("parallel", "parallel", "arbitrary"),
        name="dilated_attn",
    )(x, x, x, x, x)
    return o.reshape(b * s, A_WIDTH), lse.reshape(b * s, A_WIDTH)


def _dilated_combine_kernel(o0_ref, o1_ref, o2_ref, l0_ref, l1_ref, l2_ref, w_ref, g_ref, res_ref, out_ref):
    l0, l1, l2 = l0_ref[...], l1_ref[...], l2_ref[...]
    m = jnp.maximum(jnp.maximum(l0, l1), l2)
    e0, e1, e2 = jnp.exp(l0 - m), jnp.exp(l1 - m), jnp.exp(l2 - m)
    den = e0 + e1 + e2
    o = (e0 * o0_ref[...].astype(F32) + e1 * o1_ref[...].astype(F32) + e2 * o2_ref[...].astype(F32)) / den
    y = _dot(o.astype(BF16), w_ref[...])
    out_ref[...] = res_ref[...] + _rms(y) * g_ref[...]


def _dilated_combine(os_, lses, w_o, g, res, tm=512):
    t = res.shape[0]
    tm = min(tm, t)
    d = res.shape[1]
    row = lambda width: pl.BlockSpec((tm, width), lambda i: (i, 0))
    g = g.reshape(1, d)
    return pl.pallas_call(
        _dilated_combine_kernel,
        grid=(t // tm,),
        in_specs=[row(A_WIDTH)] * 6 + [pl.BlockSpec(w_o.shape, lambda i: (0, 0)),
                                       pl.BlockSpec((1, d), lambda i: (0, 0)), row(d)],
        out_specs=row(d),
        out_shape=jax.ShapeDtypeStruct((t, d), F32),
        compiler_params=_cparams("parallel"),
        name="dilated_combine",
    )(*os_, *lses, w_o, g, res)


def _mixer_dilated(x, tabs_a, g_pre, g_post, w_qkv, w_o):
    b, s, d = x.shape
    xf = x.reshape(b * s, d)
    ngrp = len(A_GROUPS)
    rope = (tabs_a, dict(rope_blocks=2 * ngrp, scale_blocks=ngrp, scale=A_HEAD_DIM ** -0.5, shift=A_ROT // 2))
    qkv = _norm_matmul(xf, g_pre, w_qkv.astype(BF16), out_dtype=BF16, tn=A_WIDTH, rope=rope)
    qkv = qkv.reshape(b, s, 3 * ngrp * A_WIDTH)
    os_, lses = zip(*[_dilated_attn(qkv, g, dil) for g, (_, dil) in enumerate(A_GROUPS)])
    return _dilated_combine(os_, lses, w_o.astype(BF16), g_post, xf).reshape(b, s, d)


def _gla_kernel(q_ref, k_ref, v_ref, go_ref, glr_ref, w2_ref, bg_ref, on_ref, o_ref, st_ref):
    i = pl.program_id(2)
    tc = q_ref.shape[0]
    c_len = B_CHUNK

    @pl.when(i == 0)
    def _():
        st_ref[...] = jnp.zeros_like(st_ref)

    z = _dot(glr_ref[...].astype(BF16), w2_ref[...]) + bg_ref[...]
    log_a = -_softplus(-z) * (1.0 / B_GATE_NORMALIZER)
    row = lax.broadcasted_iota(jnp.int32, (c_len, c_len), 0)
    col = lax.broadcasted_iota(jnp.int32, (c_len, c_len), 1)
    causal = col <= row
    tri = causal.astype(F32)
    scale = B_DK ** -0.5
    for c in range(tc // c_len):
        rows = slice(c * c_len, (c + 1) * c_len)
        gcum = _dot(tri, log_a[rows], precision=HIGHEST)
        g_end = gcum[c_len - 1:c_len]
        q = q_ref[rows, :].astype(F32)
        k = k_ref[rows, :].astype(F32)
        v = v_ref[rows, :]
        q_dec = (q * scale * jnp.exp(gcum)).astype(BF16)
        k_inv = (k * jnp.exp(-gcum)).astype(BF16)
        k_end = (k * jnp.exp(g_end - gcum)).astype(BF16)
        att = jnp.where(causal, _dot_t(q_dec, k_inv), 0.0).astype(BF16)
        st = st_ref[...]
        o = _dot(att, v) + _dot_t(q_dec, st.astype(BF16))
        st_ref[...] = st * jnp.exp(g_end) + _tdot(v, k_end)
        go = go_ref[rows, :].astype(F32)
        o = _rms(o) * on_ref[...] * (go * _sigmoid(go))
        o_ref[rows, :] = o.astype(o_ref.dtype)


def _gla(proj, glr, w_gate2, b_gate, o_norm, tc=256):
    b, s, _ = proj.shape
    tc = min(tc, s)
    nkb = B_KEY_DIM // B_DK
    nvb = 2 * B_KEY_DIM // B_DV
    ngb = nvb + B_VAL_DIM // B_DV
    return pl.pallas_call(
        _gla_kernel,
        grid=(b, B_HEADS, s // tc),
        in_specs=[
            pl.BlockSpec((None, tc, B_DK), lambda bi, h, i: (bi, i, h)),
            pl.BlockSpec((None, tc, B_DK), lambda bi, h, i: (bi, i, nkb + h)),
            pl.BlockSpec((None, tc, B_DV), lambda bi, h, i: (bi, i, nvb + h)),
            pl.BlockSpec((None, tc, B_DV), lambda bi, h, i: (bi, i, ngb + h)),
            pl.BlockSpec((None, tc, LANES), lambda bi, h, i: (bi, i, 0)),
            pl.BlockSpec((LANES, B_DK), lambda bi, h, i: (0, h)),
            pl.BlockSpec((1, B_DK), lambda bi, h, i: (0, h)),
            pl.BlockSpec((1, B_DV), lambda bi, h, i: (0, 0)),
        ],
        out_specs=pl.BlockSpec((None, tc, B_DV), lambda bi, h, i: (bi, i, h)),
        out_shape=jax.ShapeDtypeStruct((b, s, B_VAL_DIM), BF16),
        scratch_shapes=[pltpu.VMEM((B_DV, B_DK), F32)],
        compiler_params=_cparams("parallel", "parallel", "arbitrary"),
        name="gla",
    )(proj, proj, proj, proj, glr, w_gate2, b_gate, o_norm)


def _mixer_gla(x, g_pre, g_post, w_in, w_gate2, b_gate, o_norm, w_o):
    b, s, d = x.shape
    xf = x.reshape(b * s, d)
    main = 2 * B_KEY_DIM + 2 * B_VAL_DIM
    w_main = w_in[:, :main].astype(BF16)
    w_lr = jnp.pad(w_in[:, main:], ((0, 0), (0, LANES - B_GATE_RANK))).astype(BF16)
    proj = _norm_matmul(xf, g_pre, w_main, out_dtype=BF16, tn=main // 4)
    glr = _norm_matmul(xf, g_pre, w_lr, out_dtype=F32, tn=LANES)
    w2 = jnp.pad(w_gate2, ((0, LANES - B_GATE_RANK), (0, 0))).astype(BF16)
    og = _gla(proj.reshape(b, s, main), glr.reshape(b, s, LANES), w2, b_gate.reshape(1, B_KEY_DIM),
              o_norm.reshape(1, B_DV))
    return _matmul_norm_res(og.reshape(b * s, B_VAL_DIM), w_o.astype(BF16), g_post, xf).reshape(b, s, d)


def _mla_proj_kernel(c_ref, qn_ref, kvn_ref, wq_ref, wk_ref, wv_ref, cc_ref, s1_ref, s2_ref,
                     q_ref, k_ref, v_ref):
    c = c_ref[...]
    ckv = c[:, :C_KV_RANK]
    kpe = c[:, C_KV_RANK:C_KV_RANK + C_SLOT]
    cq = c[:, C_KV_RANK + C_SLOT:]
    cc, s1, s2 = cc_ref[...], s1_ref[...], s2_ref[...]
    shift = C_ROPE // 2
    scale = (C_NOPE + C_ROPE) ** -0.5
    q = _dot((_rms(cq) * qn_ref[...]).astype(BF16), wq_ref[...])
    ckvn = (_rms(ckv) * kvn_ref[...]).astype(BF16)
    kn = _dot(ckvn, wk_ref[...])
    kpe = _rope_tile(kpe, cc, s1, s2, shift)
    for h in range(C_HEADS):
        sl = slice(h * C_SLOT, (h + 1) * C_SLOT)
        q_ref[:, sl] = (_rope_tile(q[:, sl], cc, s1, s2, shift) * scale).astype(BF16)
        k_ref[:, sl] = (kn[:, sl] + kpe).astype(BF16)
    v_ref[...] = _dot(ckvn, wv_ref[...]).astype(BF16)


def _mla_proj(c, q_norm, kv_norm, wq, wk, wv, tabs, tm=512):
    t = c.shape[0]
    tm = min(tm, t)
    row = lambda width: pl.BlockSpec((tm, width), lambda i: (i, 0))
    full = lambda a: pl.BlockSpec(a.shape, lambda i: (0, 0))
    qn, kvn = q_norm.reshape(1, C_Q_RANK), kv_norm.reshape(1, C_KV_RANK)
    hw = C_HEADS * C_SLOT
    return pl.pallas_call(
        _mla_proj_kernel,
        grid=(t // tm,),
        in_specs=[row(c.shape[1]), full(qn), full(kvn), full(wq), full(wk), full(wv)] + [row(LANES)] * 3,
        out_specs=[row(hw), row(hw), row(C_HEADS * C_VDIM)],
        out_shape=[jax.ShapeDtypeStruct((t, hw), BF16), jax.ShapeDtypeStruct((t, hw), BF16),
                   jax.ShapeDtypeStruct((t, C_HEADS * C_VDIM), BF16)],
        compiler_params=_cparams("parallel"),
        name="mla_proj",
    )(c, qn, kvn, wq, wk, wv, *tabs)


def _mla_flash_kernel(q_ref, k_ref, v_ref, o_ref, m_ref, l_ref, acc_ref, *, tk):
    i = pl.program_id(2)
    tq = q_ref.shape[0]
    assert tq == tk
    qi = lax.broadcasted_iota(jnp.int32, (tq, tk), 0)
    kj = lax.broadcasted_iota(jnp.int32, (tq, tk), 1)
    diag = kj <= qi
    lane = lax.broadcasted_iota(jnp.int32, (tq, 2 * C_VDIM), 1)
    res = []
    for h in range(2):
        q = q_ref[:, h * C_SLOT:(h + 1) * C_SLOT]
        m_ref[...] = jnp.full_like(m_ref, NEG_INF)
        l_ref[...] = jnp.zeros_like(l_ref)
        acc_ref[...] = jnp.zeros_like(acc_ref)

        def step(j, masked, q=q, h=h):
            start = pl.multiple_of(j * tk, tk)
            k = k_ref[pl.ds(start, tk), h * C_SLOT:(h + 1) * C_SLOT]
            v = v_ref[pl.ds(start, tk), :]
            s = _dot_t(q, k)
            if masked:
                s = jnp.where(diag, s, NEG_INF)
            m_old = m_ref[...]
            m_new = jnp.maximum(m_old, jnp.max(s, axis=-1, keepdims=True))
            alpha = jnp.exp(m_old - m_new)
            p = jnp.exp(s - m_new)
            l_ref[...] = alpha * l_ref[...] + jnp.sum(p, axis=-1, keepdims=True)
            acc_ref[...] = alpha * acc_ref[...] + _dot(p.astype(BF16), v)
            m_ref[...] = m_new

        def body(j, carry):
            step(j, False)
            return carry

        lax.fori_loop(0, i, body, 0)
        step(i, True)
        res.append(acc_ref[...] / l_ref[...])
    o_ref[...] = jnp.where(lane < C_VDIM, res[0], res[1]).astype(o_ref.dtype)


def _mla_flash(q, k, v, tq=512):
    b, s, _ = q.shape
    tq = min(tq, s)
    pairs = C_HEADS // 2
    return pl.pallas_call(
        functools.partial(_mla_flash_kernel, tk=tq),
        grid=(b, pairs, s // tq),
        in_specs=[
            pl.BlockSpec((None, tq, 2 * C_SLOT), lambda bi, p, i: (bi, i, p)),
            pl.BlockSpec((None, s, 2 * C_SLOT), lambda bi, p, i: (bi, 0, p)),
            pl.BlockSpec((None, s, 2 * C_VDIM), lambda bi, p, i: (bi, 0, p)),
        ],
        out_specs=pl.BlockSpec((None, tq, 2 * C_VDIM), lambda bi, p, i: (bi, i, p)),
        out_shape=jax.ShapeDtypeStruct((b, s, C_HEADS * C_VDIM), BF16),
        scratch_shapes=[pltpu.VMEM((tq, 1), F32), pltpu.VMEM((tq, 1), F32), pltpu.VMEM((tq, 2 * C_VDIM), F32)],
        compiler_params=_cparams("parallel", "parallel", "arbitrary"),
        name="mla_flash",
    )(q, k, v)


def _mixer_mla(x, tabs_c, g_pre, g_post, w_in, q_norm, w_uq, kv_norm, w_ukv, w_o):
    b, s, d = x.shape
    xf = x.reshape(b * s, d)
    kpe_w = jnp.pad(w_in[:, C_Q_RANK + C_KV_RANK:], ((0, 0), (C_NOPE, C_SLOT - C_NOPE - C_ROPE)))
    w_c = jnp.concatenate([w_in[:, C_Q_RANK:C_Q_RANK + C_KV_RANK], kpe_w, w_in[:, :C_Q_RANK]], axis=1)
    c = _norm_matmul(xf, g_pre, w_c.astype(BF16), out_dtype=F32, tn=w_c.shape[1])
    pad = C_SLOT - C_NOPE - C_ROPE
    wq = jnp.pad(w_uq.reshape(C_Q_RANK, C_HEADS, C_NOPE + C_ROPE), ((0, 0), (0, 0), (0, pad)))
    wkv = w_ukv.reshape(C_KV_RANK, C_HEADS, C_NOPE + C_VDIM)
    wk = jnp.pad(wkv[:, :, :C_NOPE], ((0, 0), (0, 0), (0, C_SLOT - C_NOPE)))
    wv = wkv[:, :, C_NOPE:]
    q, k, v = _mla_proj(c, q_norm, kv_norm, wq.reshape(C_Q_RANK, -1).astype(BF16),
                        wk.reshape(C_KV_RANK, -1).astype(BF16), wv.reshape(C_KV_RANK, -1).astype(BF16), tabs_c)
    hw = C_HEADS * C_SLOT
    o = _mla_flash(q.reshape(b, s, hw), k.reshape(b, s, hw), v.reshape(b, s, C_HEADS * C_VDIM))
    return _matmul_norm_res(o.reshape(b * s, C_HEADS * C_VDIM), w_o.astype(BF16), g_post, xf).reshape(b, s, d)


def _rwkv_proj_kernel(x_ref, gpre_ref, mix_ref, wr_ref, wk_ref, wv_ref, w1_ref, w2_ref, a1_ref, a2_ref,
                      g1_ref, g2_ref, w0_ref, a0_ref,
                      r_ref, k_ref, v_ref, lw_ref, a_ref, g_ref, hbuf_ref, carry_ref):
    i = pl.program_id(1)
    tm = x_ref.shape[0]
    hn = _rms(x_ref[...]) * gpre_ref[...]
    @pl.when(i == 0)
    def _():
        hbuf_ref[0:SUBLANES, :] = jnp.zeros((SUBLANES, hn.shape[1]), F32)

    @pl.when(i > 0)
    def _():
        hbuf_ref[0:SUBLANES, :] = carry_ref[...]

    hbuf_ref[SUBLANES:SUBLANES + tm, :] = hn
    carry_ref[...] = hn[tm - SUBLANES:, :]
    xx = hbuf_ref[SUBLANES - 1:SUBLANES - 1 + tm, :] - hn
    mix = mix_ref[...]
    mixed = lambda n: (hn + xx * mix[n:n + 1]).astype(BF16)
    r_ref[...] = _dot(mixed(0), wr_ref[...])
    k_ref[...] = _dot(mixed(2), wk_ref[...])
    v_ref[...] = _dot(mixed(3), wv_ref[...])
    wl = _dot(jnp.tanh(_dot(mixed(1), w1_ref[...])).astype(BF16), w2_ref[...])
    w = -_softplus(-(w0_ref[...] + wl)) - 0.5
    lw_ref[...] = -jnp.exp(w)
    al = _dot(_dot(mixed(4), a1_ref[...]).astype(BF16), a2_ref[...])
    a_ref[...] = _sigmoid(a0_ref[...] + al)
    g_ref[...] = _dot(_sigmoid(_dot(mixed(5), g1_ref[...])).astype(BF16), g2_ref[...]).astype(g_ref.dtype)


def _rwkv_proj(x, gpre, mix, w_rkv, w1, w2, a1, a2, g1, g2, w0, a0, tm=256):
    b, s, d = x.shape
    tm = min(tm, s)
    const = lambda a: pl.BlockSpec(a.shape, lambda bi, i: (0,) * a.ndim)
    xspec = pl.BlockSpec((None, tm, d), lambda bi, i: (bi, i, 0))
    bf = lambda a: a.astype(BF16)
    args = [gpre.reshape(1, d), mix, bf(w_rkv[0]), bf(w_rkv[1]), bf(w_rkv[2]), bf(w1), bf(w2), bf(a1), bf(a2),
            bf(g1), bf(g2), w0.reshape(1, d), a0.reshape(1, d)]
    f32_out = jax.ShapeDtypeStruct(x.shape, F32)
    return pl.pallas_call(
        _rwkv_proj_kernel,
        grid=(b, s // tm),
        in_specs=[xspec] + [const(a) for a in args],
        out_specs=[xspec] * 6,
        out_shape=[f32_out] * 5 + [jax.ShapeDtypeStruct(x.shape, BF16)],
        scratch_shapes=[pltpu.VMEM((tm + SUBLANES, d), F32), pltpu.VMEM((SUBLANES, d), F32)],
        compiler_params=_cparams("arbitrary", "arbitrary"),
        name="rwkv_proj",
    )(x, *args)


def _rwkv_chunk(r, k, v, lw, a, kk_w, ka_w, masks, st):
    tri_incl, strict, incl, blk, eye = masks
    hp = dict(precision=HIGHEST)
    ln = r.shape[0]
    gc = _dot(tri_incl, lw, **hp)
    gp = gc - lw
    g_end = gc[ln - 1:ln]
    kk = k * kk_w
    kk = kk / jnp.maximum(jnp.sqrt(jnp.sum(kk * kk, axis=-1, keepdims=True)), 1e-12)
    k_mod = k * (1.0 + (a - 1.0) * ka_w)
    beta = kk * a
    e_neg = jnp.exp(-gc)
    e_end = jnp.exp(g_end - gc)
    a_t = -kk * jnp.exp(gp)
    r_t = r * jnp.exp(gc)
    lhs = jnp.concatenate([a_t, r_t], axis=0)
    rhs = jnp.concatenate([beta * e_neg, k_mod * e_neg], axis=0)
    sc = _dot_t(lhs, rhs, **hp)
    n_ab = jnp.where(strict, sc[:ln, :ln], 0.0)
    a_ak = jnp.where(strict, sc[:ln, ln:], 0.0)
    a_rb = jnp.where(incl, sc[ln:, :ln], 0.0)
    a_rk = jnp.where(incl, sc[ln:, ln:], 0.0)
    n_d = jnp.where(blk, n_ab, 0.0)
    n_o = n_ab - n_d
    p = eye + n_d
    pw = n_d
    for _ in range(int(math.log2(D_SUB)) - 1):
        pw = _dot(pw, pw, **hp)
        p = p + _dot(p, pw, **hp)
    m1 = _dot(p, n_o, **hp)
    m2 = _dot(m1, m1, **hp)
    assert D_CHUNK // D_SUB == 4
    w12 = _dot(p, jnp.concatenate([a_t, _dot(a_ak, v, **hp)], axis=1), **hp)
    w12 = w12 + _dot(m1, w12, **hp)
    w12 = w12 + _dot(m2, w12, **hp)
    qy = _dot(a_rb, w12, **hp)
    q_eff = r_t + qy[:, :D_HEAD]
    y0 = qy[:, D_HEAD:] + _dot(a_rk, v, **hp)
    w1, w2 = w12[:, :D_HEAD], w12[:, D_HEAD:]
    b_end, k_end = beta * e_end, k_mod * e_end
    trans = eye * jnp.exp(g_end) + _tdot(w1, b_end, **hp)
    s_add = _tdot(jnp.concatenate([w2, v], axis=0), jnp.concatenate([b_end, k_end], axis=0), **hp)
    y = _dot_t(q_eff, st, **hp) + y0
    st_new = _dot(st, trans, **hp) + s_add
    return y, st_new, k_mod


def _rwkv_rec_kernel(r_ref, k_ref, v_ref, lw_ref, a_ref, kk_ref, ka_ref, rk_ref, lnw_ref, lnb_ref,
                     o_ref, st_ref):
    i = pl.program_id(2)
    tc = r_ref.shape[0]
    ln = D_CHUNK

    @pl.when(i == 0)
    def _():
        st_ref[...] = jnp.zeros_like(st_ref)

    row = lax.broadcasted_iota(jnp.int32, (ln, ln), 0)
    col = lax.broadcasted_iota(jnp.int32, (ln, ln), 1)
    incl, strict = col <= row, col < row
    masks = (incl.astype(F32), strict, incl, (row // D_SUB) == (col // D_SUB), (row == col).astype(F32))
    for c in range(tc // ln):
        rows = slice(c * ln, (c + 1) * ln)
        outs = []
        for h in range(LANES // D_HEAD):
            sl = slice(h * D_HEAD, (h + 1) * D_HEAD)
            r, k, v = r_ref[rows, sl], k_ref[rows, sl], v_ref[rows, sl]
            y, st_new, k_mod = _rwkv_chunk(r, k, v, lw_ref[rows, sl], a_ref[rows, sl], kk_ref[:, sl], ka_ref[:, sl],
                                           masks, st_ref[h])
            st_ref[h] = st_new
            mu = jnp.mean(y, axis=-1, keepdims=True)
            yc = y - mu
            var = jnp.mean(yc * yc, axis=-1, keepdims=True)
            gn = yc * lax.rsqrt(var + D_GN_EPS) * lnw_ref[:, sl] + lnb_ref[:, sl]
            bonus = jnp.sum(r * k_mod * rk_ref[:, sl], axis=-1, keepdims=True) * v
            outs.append(gn + bonus)
        o_ref[rows, :] = jnp.concatenate(outs, axis=1)


def _rwkv_rec(r, k, v, lw, a, k_k, k_a, r_k, lnx_w, lnx_b, tc=128):
    b, s, d = r.shape
    tc = min(tc, s)
    blk = pl.BlockSpec((None, tc, LANES), lambda bi, p, i: (bi, i, p))
    par = pl.BlockSpec((1, LANES), lambda bi, p, i: (0, p))
    params = [t.reshape(1, d) for t in (k_k, k_a, r_k, lnx_w, lnx_b)]
    return pl.pallas_call(
        _rwkv_rec_kernel,
        grid=(b, d // LANES, s // tc),
        in_specs=[blk] * 5 + [par] * 5,
        out_specs=blk,
        out_shape=jax.ShapeDtypeStruct(r.shape, F32),
        scratch_shapes=[pltpu.VMEM((LANES // D_HEAD, D_HEAD, D_HEAD), F32)],
        compiler_params=_cparams("parallel", "parallel", "arbitrary"),
        name="rwkv_rec",
    )(r, k, v, lw, a, *params)


def _mixer_rwkv7(x, g_pre, g_post, mix, w_rkv, w0, w1, w2, a0, a1, a2, g1, g2, k_k, k_a, r_k, lnx_w, lnx_b, w_o):
    b, s, d = x.shape
    r, k, v, lw, a, g = _rwkv_proj(x, g_pre, mix, w_rkv, w1, w2, a1, a2, g1, g2, w0, a0)
    y = _rwkv_rec(r, k, v, lw, a, k_k, k_a, r_k, lnx_w, lnx_b)
    flat = lambda t: t.reshape(b * s, d)
    return _matmul_norm_res(flat(y), w_o.astype(BF16), g_post, flat(x), gate=flat(g)).reshape(b, s, d)


def kernel(x, mem, positions, ln_gains, mem_norm, mem_w_kv, mem_w_q, mem_w_o, ffn_w_in, ffn_conv_w, ffn_conv_b, ffn_w_out, a_w_qkv, a_w_o, b_w_in, b_w_gate2, b_gate_bias, b_o_norm, b_w_o, c_w_in, c_q_norm, c_w_uq, c_kv_norm, c_w_ukv, c_w_o, d_mix, d_w_rkv, d_w0, d_w1, d_w2, d_a0, d_a1, d_a2, d_g1, d_g2, d_k_k, d_k_a, d_r_k, d_lnx_w, d_lnx_b, d_w_o):
    b, s, d = x.shape
    depth = ln_gains.shape[0]
    mlen = mem.shape[1]
    mwidth = M_HEADS * M_HEAD_DIM
    mkv = _norm_matmul(mem.reshape(b * mlen, d), mem_norm, mem_w_kv.astype(BF16), out_dtype=BF16,
                       tn=2 * mwidth, tm=mlen)
    mkv = mkv.reshape(b, mlen, 2 * mwidth)
    mem_k, mem_v = mkv[:, :, :mwidth], mkv[:, :, mwidth:]
    tabs = _rope_tables(positions)
    tabs_a, tabs_c = tabs[:3], tabs[3:]
    for i in range(depth):
        m, j = i % N_MIXERS, i // N_MIXERS
        gains = ln_gains[i]
        if m == 0:
            x = _mixer_dilated(x, tabs_a, gains[0], gains[1], a_w_qkv[j], a_w_o[j])
        elif m == 1:
            x = _mixer_gla(x, gains[0], gains[1], b_w_in[j], b_w_gate2[j], b_gate_bias[j], b_o_norm[j], b_w_o[j])
        elif m == 2:
            x = _mixer_mla(x, tabs_c, gains[0], gains[1], c_w_in[j], c_q_norm[j], c_w_uq[j], c_kv_norm[j],
                           c_w_ukv[j], c_w_o[j])
        else:
            x = _mixer_rwkv7(x, gains[0], gains[1], d_mix[j], d_w_rkv[j], d_w0[j], d_w1[j], d_w2[j], d_a0[j],
                             d_a1[j], d_a2[j], d_g1[j], d_g2[j], d_k_k[j], d_k_a[j], d_r_k[j], d_lnx_w[j],
                             d_lnx_b[j], d_w_o[j])
        x = _mem_attn(x, gains[2], mem_w_q[i].astype(BF16), mem_k, mem_v, mem_w_o[i].astype(BF16), gains[3])
        x = _conv_ffn(x, gains[4], ffn_w_in[i], ffn_conv_w[i], ffn_conv_b[i], ffn_w_out[i], gains[5])
    return x
```

```python
import functools
import math

import jax
import jax.numpy as jnp
from jax import lax
from jax.experimental import pallas as pl
from jax.experimental.pallas import tpu as pltpu

F32 = jnp.float32
BF16 = jnp.bfloat16
HIGHEST = lax.Precision.HIGHEST

D_MODEL = 1024
N_MIXERS = 4
NORM_EPS = 1e-6
ROPE_THETA = 500000.0
NEG_INF = -1e30

A_HEAD_DIM = 64
A_HEADS = 8
A_GROUPS = ((128, 1), (512, 4), (2048, 16))
A_BLOCK = 128
A_ROT = A_HEAD_DIM // 4
A_WIDTH = A_HEADS * A_HEAD_DIM

B_HEADS = 4
B_KEY_DIM = 512
B_VAL_DIM = 1024
B_DK = B_KEY_DIM // B_HEADS
B_DV = B_VAL_DIM // B_HEADS
B_GATE_RANK = 16
B_GATE_NORMALIZER = 16.0
B_CHUNK = 64

C_HEADS = 16
C_Q_RANK = 384
C_KV_RANK = 256
C_NOPE = 64
C_ROPE = 32
C_VDIM = 64
C_SLOT = 128

D_HEAD = 64
D_HEADS = D_MODEL // D_HEAD
D_GN_EPS = 64e-5
D_CHUNK = 64
D_SUB = 16

M_HEADS = 4
M_HEAD_DIM = 128

D_FF = 2816
CONV_WIDTH = 3
FF_CHUNK = 256

LANES = 128
SUBLANES = 8
VMEM_LIMIT = 56 * 1024 * 1024


def _cparams(*sem):
    return pltpu.CompilerParams(dimension_semantics=sem, vmem_limit_bytes=VMEM_LIMIT)


def _rms(x, eps=NORM_EPS):
    return x * lax.rsqrt(jnp.mean(x * x, axis=-1, keepdims=True) + eps)


def _sigmoid(x):
    return 1.0 / (1.0 + jnp.exp(-x))


def _softplus(x):
    return jnp.maximum(x, 0.0) + jnp.log(1.0 + jnp.exp(-jnp.abs(x)))


def _dot(a, b, **kw):
    return jnp.dot(a, b, preferred_element_type=F32, **kw)


def _dot_t(a, b, **kw):
    return lax.dot_general(a, b, (((1,), (1,)), ((), ())), preferred_element_type=F32, **kw)


def _tdot(a, b, **kw):
    return lax.dot_general(a, b, (((0,), (0,)), ((), ())), preferred_element_type=F32, **kw)


def _rope_tile(y, c, s1, s2, shift):
    return y * c + pltpu.roll(y, LANES - shift, 1) * s1 + pltpu.roll(y, shift, 1) * s2


def _rope_table_kernel(pos_ref, ca_ref, sa1_ref, sa2_ref, cc_ref, sc1_ref, sc2_ref):
    pos = pos_ref[...].astype(F32)
    lane = lax.broadcasted_iota(jnp.int32, (1, LANES), 1)
    log_theta = math.log(ROPE_THETA)

    def tables(rel, half, rot):
        in_span = (rel >= 0) & (rel < rot)
        idx = jnp.where(in_span, jnp.where(rel < half, rel, rel - half), 0).astype(F32)
        inv_freq = jnp.exp(-(idx * (2.0 / rot)) * log_theta)
        ang = pos * inv_freq
        cos, sin = jnp.cos(ang), jnp.sin(ang)
        first = (rel >= 0) & (rel < half)
        second = (rel >= half) & (rel < rot)
        return (jnp.where(in_span, cos, 1.0), jnp.where(first, -sin, 0.0), jnp.where(second, sin, 0.0))

    ca, sa1, sa2 = tables(lane % A_HEAD_DIM, A_ROT // 2, A_ROT)
    cc, sc1, sc2 = tables(lane - C_NOPE, C_ROPE // 2, C_ROPE)
    ca_ref[...], sa1_ref[...], sa2_ref[...] = ca, sa1, sa2
    cc_ref[...], sc1_ref[...], sc2_ref[...] = cc, sc1, sc2


def _rope_tables(positions):
    t = positions.size
    tm = min(t, 1024)
    pos = positions.reshape(t, 1)
    out = jax.ShapeDtypeStruct((t, LANES), F32)
    spec = pl.BlockSpec((tm, LANES), lambda i: (i, 0))
    return pl.pallas_call(
        _rope_table_kernel,
        grid=(t // tm,),
        in_specs=[pl.BlockSpec((tm, 1), lambda i: (i, 0))],
        out_specs=[spec] * 6,
        out_shape=[out] * 6,
        compiler_params=_cparams("parallel"),
        name="rope_tables",
    )(pos)


def _norm_matmul_kernel(*refs, rope_blocks, scale_blocks, scale, shift):
    if rope_blocks:
        x_ref, g_ref, w_ref, c_ref, s1_ref, s2_ref, o_ref, xn_ref = refs
    else:
        x_ref, g_ref, w_ref, o_ref, xn_ref = refs
    j = pl.program_id(1)

    @pl.when(j == 0)
    def _():
        xn_ref[...] = (_rms(x_ref[...]) * g_ref[...]).astype(BF16)

    y = _dot(xn_ref[...], w_ref[...])
    if rope_blocks:
        on = j < rope_blocks
        c = jnp.where(on, c_ref[...], 1.0)
        s1 = jnp.where(on, s1_ref[...], 0.0)
        s2 = jnp.where(on, s2_ref[...], 0.0)
        sc = jnp.where(j < scale_blocks, scale, 1.0)
        tn = y.shape[1]
        y = jnp.concatenate(
            [_rope_tile(y[:, s * LANES:(s + 1) * LANES], c, s1, s2, shift) for s in range(tn // LANES)], axis=1) * sc
    o_ref[...] = y.astype(o_ref.dtype)


def _norm_matmul(x, g, w, *, out_dtype, tn, tm=1024, k=None, k_block=0, rope=None):
    t = x.shape[0]
    k = x.shape[1] if k is None else k
    n = w.shape[1]
    tm = min(tm, t)
    in_specs = [
        pl.BlockSpec((tm, k), lambda i, j: (i, k_block)),
        pl.BlockSpec((1, k), lambda i, j: (0, 0)),
        pl.BlockSpec((k, tn), lambda i, j: (0, j)),
    ]
    args = [x, g.reshape(1, k), w]
    kw = dict(rope_blocks=0, scale_blocks=0, scale=1.0, shift=0)
    if rope is not None:
        tabs, kw = rope
        in_specs += [pl.BlockSpec((tm, LANES), lambda i, j: (i, 0))] * 3
        args += list(tabs)
    return pl.pallas_call(
        functools.partial(_norm_matmul_kernel, **kw),
        grid=(t // tm, n // tn),
        in_specs=in_specs,
        out_specs=pl.BlockSpec((tm, tn), lambda i, j: (i, j)),
        out_shape=jax.ShapeDtypeStruct((t, n), out_dtype),
        scratch_shapes=[pltpu.VMEM((tm, k), BF16)],
        compiler_params=_cparams("parallel", "arbitrary"),
        name="norm_matmul",
    )(*args)


def _matmul_norm_res_kernel(*refs, gated):
    if gated:
        h_ref, gate_ref, w_ref, g_ref, res_ref, o_ref = refs
        h = (h_ref[...] * gate_ref[...].astype(F32)).astype(BF16)
    else:
        h_ref, w_ref, g_ref, res_ref, o_ref = refs
        h = h_ref[...]
    y = _dot(h, w_ref[...])
    o_ref[...] = res_ref[...] + _rms(y) * g_ref[...]


def _matmul_norm_res(h, w, g, res, gate=None, tm=512):
    t, k = h.shape
    n = w.shape[1]
    tm = min(tm, t)
    row = lambda width: pl.BlockSpec((tm, width), lambda i: (i, 0))
    full = lambda a: pl.BlockSpec(a.shape, lambda i: (0, 0))
    g2 = g.reshape(1, n)
    if gate is None:
        args, specs = [h, w, g2, res], [row(k), full(w), full(g2), row(n)]
    else:
        args, specs = [h, gate, w, g2, res], [row(k), row(k), full(w), full(g2), row(n)]
    return pl.pallas_call(
        functools.partial(_matmul_norm_res_kernel, gated=gate is not None),
        grid=(t // tm,),
        in_specs=specs,
        out_specs=row(n),
        out_shape=jax.ShapeDtypeStruct((t, n), F32),
        compiler_params=_cparams("parallel"),
        name="matmul_norm_res",
    )(*args)


def _mem_attn_kernel(x_ref, gpre_ref, wq_ref, k_ref, v_ref, wo_ref, gpost_ref, o_ref):
    x = x_ref[...]
    hn = (_rms(x) * gpre_ref[...]).astype(BF16)
    q = (_dot(hn, wq_ref[...]) * (M_HEAD_DIM ** -0.5)).astype(BF16)
    k, v = k_ref[...], v_ref[...]
    outs = []
    for h in range(M_HEADS):
        sl = slice(h * M_HEAD_DIM, (h + 1) * M_HEAD_DIM)
        s = _dot_t(q[:, sl], k[:, sl])
        p = jnp.exp(s - jnp.max(s, axis=-1, keepdims=True))
        p = p / jnp.sum(p, axis=-1, keepdims=True)
        outs.append(_dot(p.astype(BF16), v[:, sl]))
    o = jnp.concatenate(outs, axis=1).astype(BF16)
    y = _dot(o, wo_ref[...])
    o_ref[...] = x + _rms(y) * gpost_ref[...]


def _mem_attn(x, gpre, wq, mem_k, mem_v, wo, gpost, tm=512):
    b, s, d = x.shape
    tm = min(tm, s)
    width = M_HEADS * M_HEAD_DIM
    mlen = mem_k.shape[1]
    const = lambda a: pl.BlockSpec(a.shape, lambda bi, i: (0,) * a.ndim)
    gpre, gpost = gpre.reshape(1, d), gpost.reshape(1, d)
    xspec = pl.BlockSpec((None, tm, d), lambda bi, i: (bi, i, 0))
    mspec = pl.BlockSpec((None, mlen, width), lambda bi, i: (bi, 0, 0))
    return pl.pallas_call(
        _mem_attn_kernel,
        grid=(b, s // tm),
        in_specs=[xspec, const(gpre), const(wq), mspec, mspec, const(wo), const(gpost)],
        out_specs=xspec,
        out_shape=jax.ShapeDtypeStruct(x.shape, F32),
        compiler_params=_cparams("parallel", "parallel"),
        name="mem_attn",
    )(x, gpre, wq, mem_k, mem_v, wo, gpost)


def _conv_ffn_kernel(x_ref, gpre_ref, win_ref, cw_ref, cb_ref, wout_ref, gpost_ref, o_ref,
                     hn_ref, acc_ref, ubuf_ref, carry_ref):
    i, c = pl.program_id(1), pl.program_id(2)
    tm = x_ref.shape[0]
    fc = wout_ref.shape[0]

    @pl.when(c == 0)
    def _():
        hn_ref[...] = (_rms(x_ref[...]) * gpre_ref[...]).astype(BF16)
        acc_ref[...] = jnp.zeros_like(acc_ref)

    u = _dot(hn_ref[...], win_ref[...])
    @pl.when(i == 0)
    def _():
        ubuf_ref[0:SUBLANES, :] = jnp.zeros((SUBLANES, 2 * fc), F32)

    @pl.when(i > 0)
    def _():
        ubuf_ref[0:SUBLANES, :] = carry_ref[c]

    ubuf_ref[SUBLANES:SUBLANES + tm, :] = u
    carry_ref[c] = u[tm - SUBLANES:, :]
    cw = cw_ref[...]
    conv = (cb_ref[...] + ubuf_ref[SUBLANES - 2:SUBLANES - 2 + tm, :] * cw[0:1]
            + ubuf_ref[SUBLANES - 1:SUBLANES - 1 + tm, :] * cw[1:2] + u * cw[2:3])
    gate, val = conv[:, :fc], conv[:, fc:]
    h = (gate * _sigmoid(gate) * val).astype(BF16)
    acc_ref[...] += _dot(h, wout_ref[...])

    @pl.when(c == pl.num_programs(2) - 1)
    def _():
        o_ref[...] = x_ref[...] + _rms(acc_ref[...]) * gpost_ref[...]


def _conv_ffn(x, gpre, w_in, conv_w, conv_b, w_out, gpost, tm=1024):
    b, s, d = x.shape
    tm = min(tm, s)
    fc = FF_CHUNK
    nc = D_FF // fc

    def interleave(a):
        lead = a.shape[:-1]
        a = a.reshape(lead + (2, nc, fc))
        return jnp.swapaxes(a, -3, -2).reshape(lead + (2 * D_FF,))

    w_in = interleave(w_in).astype(BF16)
    conv_w = interleave(conv_w)
    conv_b = interleave(conv_b).reshape(1, 2 * D_FF)
    w_out = w_out.astype(BF16)
    gpre, gpost = gpre.reshape(1, d), gpost.reshape(1, d)
    xspec = pl.BlockSpec((None, tm, d), lambda bi, i, c: (bi, i, 0))
    gspec = pl.BlockSpec((1, d), lambda bi, i, c: (0, 0))
    return pl.pallas_call(
        _conv_ffn_kernel,
        grid=(b, s // tm, nc),
        in_specs=[
            xspec, gspec,
            pl.BlockSpec((d, 2 * fc), lambda bi, i, c: (0, c)),
            pl.BlockSpec((CONV_WIDTH, 2 * fc), lambda bi, i, c: (0, c)),
            pl.BlockSpec((1, 2 * fc), lambda bi, i, c: (0, c)),
            pl.BlockSpec((fc, d), lambda bi, i, c: (c, 0)),
            gspec,
        ],
        out_specs=xspec,
        out_shape=jax.ShapeDtypeStruct(x.shape, F32),
        scratch_shapes=[
            pltpu.VMEM((tm, d), BF16),
            pltpu.VMEM((tm, d), F32),
            pltpu.VMEM((tm + SUBLANES, 2 * fc), F32),
            pltpu.VMEM((nc, SUBLANES, 2 * fc), F32),
        ],
        compiler_params=_cparams("arbitrary", "arbitrary", "arbitrary"),
        name="conv_ffn",
    )(x, gpre, w_in, conv_w, conv_b, w_out, gpost)


def _dilated_attn_kernel(q_ref, kc_ref, kp_ref, vc_ref, vp_ref, o_ref, lse_ref):
    n = pl.program_id(2)
    q, kc, kp, vc, vp = q_ref[...], kc_ref[...], kp_ref[...], vc_ref[...], vp_ref[...]
    qi = lax.broadcasted_iota(jnp.int32, (A_BLOCK, A_BLOCK), 0)
    kj = lax.broadcasted_iota(jnp.int32, (A_BLOCK, A_BLOCK), 1)
    mask_p = (kj >= qi) & (n > 0)
    mask_c = kj <= qi
    outs, lses = [], []
    for h in range(A_HEADS):
        sl = slice(h * A_HEAD_DIM, (h + 1) * A_HEAD_DIM)
        s_p = jnp.where(mask_p, _dot_t(q[:, sl], kp[:, sl]), NEG_INF)
        s_c = jnp.where(mask_c, _dot_t(q[:, sl], kc[:, sl]), NEG_INF)
        m = jnp.maximum(jnp.max(s_p, axis=-1, keepdims=True), jnp.max(s_c, axis=-1, keepdims=True))
        p_p, p_c = jnp.exp(s_p - m), jnp.exp(s_c - m)
        l = jnp.sum(p_p, axis=-1, keepdims=True) + jnp.sum(p_c, axis=-1, keepdims=True)
        o = (_dot(p_p.astype(BF16), vp[:, sl]) + _dot(p_c.astype(BF16), vc[:, sl])) / l
        outs.append(o)
        lses.append(jnp.broadcast_to(m + jnp.log(l), (A_BLOCK, A_HEAD_DIM)))
    o_ref[...] = jnp.concatenate(outs, axis=1).astype(o_ref.dtype)
    lse_ref[...] = jnp.concatenate(lses, axis=1)


def _dilated_attn(qkv, g, dil):
    b, s, width = qkv.shape
    sections = width // A_WIDTH
    ln = s // dil
    nb = ln // A_BLOCK
    x = qkv.reshape(b, ln, dil * width)

    def spec(section, prev):
        def index(bi, r, n):
            return (bi, jnp.maximum(n - 1, 0) if prev else n, r * sections + section * len(A_GROUPS) + g)
        return pl.BlockSpec((None, A_BLOCK, A_WIDTH), index)

    ospec = pl.BlockSpec((None, A_BLOCK, A_WIDTH), lambda bi, r, n: (bi, n, r))
    o, lse = pl.pallas_call(
        _dilated_attn_kernel,
        grid=(b, dil, nb),
        in_specs=[spec(0, False), spec(1, False), spec(1, True), spec(2, False), spec(2, True)],
        out_specs=[ospec, ospec],
        out_shape=[jax.ShapeDtypeStruct((b, ln, dil * A_WIDTH), BF16),
                   jax.ShapeDtypeStruct((b, ln, dil * A_WIDTH), F32)],
        compiler_params=_cparams("parallel", "parallel", "arbitrary"),
        name="dilated_attn",
    )(x, x, x, x, x)
    return o.reshape(b * s, A_WIDTH), lse.reshape(b * s, A_WIDTH)


def _dilated_combine_kernel(o0_ref, o1_ref, o2_ref, l0_ref, l1_ref, l2_ref, w_ref, g_ref, res_ref, out_ref):
    l0, l1, l2 = l0_ref[...], l1_ref[...], l2_ref[...]
    m = jnp.maximum(jnp.maximum(l0, l1), l2)
    e0, e1, e2 = jnp.exp(l0 - m), jnp.exp(l1 - m), jnp.exp(l2 - m)
    den = e0 + e1 + e2
    o = (e0 * o0_ref[...].astype(F32) + e1 * o1_ref[...].astype(F32) + e2 * o2_ref[...].astype(F32)) / den
    y = _dot(o.astype(BF16), w_ref[...])
    out_ref[...] = res_ref[...] + _rms(y) * g_ref[...]


def _dilated_combine(os_, lses, w_o, g, res, tm=512):
    t = res.shape[0]
    tm = min(tm, t)
    d = res.shape[1]
    row = lambda width: pl.BlockSpec((tm, width), lambda i: (i, 0))
    g = g.reshape(1, d)
    return pl.pallas_call(
        _dilated_combine_kernel,
        grid=(t // tm,),
        in_specs=[row(A_WIDTH)] * 6 + [pl.BlockSpec(w_o.shape, lambda i: (0, 0)),
                                       pl.BlockSpec((1, d), lambda i: (0, 0)), row(d)],
        out_specs=row(d),
        out_shape=jax.ShapeDtypeStruct((t, d), F32),
        compiler_params=_cparams("parallel"),
        name="dilated_combine",
    )(*os_, *lses, w_o, g, res)


def _mixer_dilated(x, tabs_a, g_pre, g_post, w_qkv, w_o):
    b, s, d = x.shape
    xf = x.reshape(b * s, d)
    ngrp = len(A_GROUPS)
    rope = (tabs_a, dict(rope_blocks=2 * ngrp, scale_blocks=ngrp, scale=A_HEAD_DIM ** -0.5, shift=A_ROT // 2))
    qkv = _norm_matmul(xf, g_pre, w_qkv.astype(BF16), out_dtype=BF16, tn=A_WIDTH, rope=rope)
    qkv = qkv.reshape(b, s, 3 * ngrp * A_WIDTH)
    os_, lses = zip(*[_dilated_attn(qkv, g, dil) for g, (_, dil) in enumerate(A_GROUPS)])
    return _dilated_combine(os_, lses, w_o.astype(BF16), g_post, xf).reshape(b, s, d)


def _gla_kernel(q_ref, k_ref, v_ref, go_ref, glr_ref, w2_ref, bg_ref, on_ref, o_ref, st_ref):
    i = pl.program_id(2)
    tc = q_ref.shape[0]
    c_len = B_CHUNK

    @pl.when(i == 0)
    def _():
        st_ref[...] = jnp.zeros_like(st_ref)

    z = _dot(glr_ref[...].astype(BF16), w2_ref[...]) + bg_ref[...]
    log_a = -_softplus(-z) * (1.0 / B_GATE_NORMALIZER)
    row = lax.broadcasted_iota(jnp.int32, (c_len, c_len), 0)
    col = lax.broadcasted_iota(jnp.int32, (c_len, c_len), 1)
    causal = col <= row
    tri = causal.astype(F32)
    scale = B_DK ** -0.5
    for c in range(tc // c_len):
        rows = slice(c * c_len, (c + 1) * c_len)
        gcum = _dot(tri, log_a[rows], precision=HIGHEST)
        g_end = gcum[c_len - 1:c_len]
        q = q_ref[rows, :].astype(F32)
        k = k_ref[rows, :].astype(F32)
        v = v_ref[rows, :]
        q_dec = (q * scale * jnp.exp(gcum)).astype(BF16)
        k_inv = (k * jnp.exp(-gcum)).astype(BF16)
        k_end = (k * jnp.exp(g_end - gcum)).astype(BF16)
        att = jnp.where(causal, _dot_t(q_dec, k_inv), 0.0).astype(BF16)
        st = st_ref[...]
        o = _dot(att, v) + _dot_t(q_dec, st.astype(BF16))
        st_ref[...] = st * jnp.exp(g_end) + _tdot(v, k_end)
        go = go_ref[rows, :].astype(F32)
        o = _rms(o) * on_ref[...] * (go * _sigmoid(go))
        o_ref[rows, :] = o.astype(o_ref.dtype)


def _gla(proj, glr, w_gate2, b_gate, o_norm, tc=256):
    b, s, _ = proj.shape
    tc = min(tc, s)
    nkb = B_KEY_DIM // B_DK
    nvb = 2 * B_KEY_DIM // B_DV
    ngb = nvb + B_VAL_DIM // B_DV
    return pl.pallas_call(
        _gla_kernel,
        grid=(b, B_HEADS, s // tc),
        in_specs=[
            pl.BlockSpec((None, tc, B_DK), lambda bi, h, i: (bi, i, h)),
            pl.BlockSpec((None, tc, B_DK), lambda bi, h, i: (bi, i, nkb + h)),
            pl.BlockSpec((None, tc, B_DV), lambda bi, h, i: (bi, i, nvb + h)),
            pl.BlockSpec((None, tc, B_DV), lambda bi, h, i: (bi, i, ngb + h)),
            pl.BlockSpec((None, tc, LANES), lambda bi, h, i: (bi, i, 0)),
            pl.BlockSpec((LANES, B_DK), lambda bi, h, i: (0, h)),
            pl.BlockSpec((1, B_DK), lambda bi, h, i: (0, h)),
            pl.BlockSpec((1, B_DV), lambda bi, h, i: (0, 0)),
        ],
        out_specs=pl.BlockSpec((None, tc, B_DV), lambda bi, h, i: (bi, i, h)),
        out_shape=jax.ShapeDtypeStruct((b, s, B_VAL_DIM), BF16),
        scratch_shapes=[pltpu.VMEM((B_DV, B_DK), F32)],
        compiler_params=_cparams("parallel", "parallel", "arbitrary"),
        name="gla",
    )(proj, proj, proj, proj, glr, w_gate2, b_gate, o_norm)


def _mixer_gla(x, g_pre, g_post, w_in, w_gate2, b_gate, o_norm, w_o):
    b, s, d = x.shape
    xf = x.reshape(b * s, d)
    main = 2 * B_KEY_DIM + 2 * B_VAL_DIM
    w_main = w_in[:, :main].astype(BF16)
    w_lr = jnp.pad(w_in[:, main:], ((0, 0), (0, LANES - B_GATE_RANK))).astype(BF16)
    proj = _norm_matmul(xf, g_pre, w_main, out_dtype=BF16, tn=main // 4)
    glr = _norm_matmul(xf, g_pre, w_lr, out_dtype=F32, tn=LANES)
    w2 = jnp.pad(w_gate2, ((0, LANES - B_GATE_RANK), (0, 0))).astype(BF16)
    og = _gla(proj.reshape(b, s, main), glr.reshape(b, s, LANES), w2, b_gate.reshape(1, B_KEY_DIM),
              o_norm.reshape(1, B_DV))
    return _matmul_norm_res(og.reshape(b * s, B_VAL_DIM), w_o.astype(BF16), g_post, xf).reshape(b, s, d)


def _mla_proj_kernel(c_ref, qn_ref, kvn_ref, wq_ref, wk_ref, wv_ref, cc_ref, s1_ref, s2_ref,
                     q_ref, k_ref, v_ref):
    c = c_ref[...]
    ckv = c[:, :C_KV_RANK]
    kpe = c[:, C_KV_RANK:C_KV_RANK + C_SLOT]
    cq = c[:, C_KV_RANK + C_SLOT:]
    cc, s1, s2 = cc_ref[...], s1_ref[...], s2_ref[...]
    shift = C_ROPE // 2
    scale = (C_NOPE + C_ROPE) ** -0.5
    q = _dot((_rms(cq) * qn_ref[...]).astype(BF16), wq_ref[...])
    ckvn = (_rms(ckv) * kvn_ref[...]).astype(BF16)
    kn = _dot(ckvn, wk_ref[...])
    kpe = _rope_tile(kpe, cc, s1, s2, shift)
    for h in range(C_HEADS):
        sl = slice(h * C_SLOT, (h + 1) * C_SLOT)
        q_ref[:, sl] = (_rope_tile(q[:, sl], cc, s1, s2, shift) * scale).astype(BF16)
        k_ref[:, sl] = (kn[:, sl] + kpe).astype(BF16)
    v_ref[...] = _dot(ckvn, wv_ref[...]).astype(BF16)


def _mla_proj(c, q_norm, kv_norm, wq, wk, wv, tabs, tm=512):
    t = c.shape[0]
    tm = min(tm, t)
    row = lambda width: pl.BlockSpec((tm, width), lambda i: (i, 0))
    full = lambda a: pl.BlockSpec(a.shape, lambda i: (0, 0))
    qn, kvn = q_norm.reshape(1, C_Q_RANK), kv_norm.reshape(1, C_KV_RANK)
    hw = C_HEADS * C_SLOT
    return pl.pallas_call(
        _mla_proj_kernel,
        grid=(t // tm,),
        in_specs=[row(c.shape[1]), full(qn), full(kvn), full(wq), full(wk), full(wv)] + [row(LANES)] * 3,
        out_specs=[row(hw), row(hw), row(C_HEADS * C_VDIM)],
        out_shape=[jax.ShapeDtypeStruct((t, hw), BF16), jax.ShapeDtypeStruct((t, hw), BF16),
                   jax.ShapeDtypeStruct((t, C_HEADS * C_VDIM), BF16)],
        compiler_params=_cparams("parallel"),
        name="mla_proj",
    )(c, qn, kvn, wq, wk, wv, *tabs)


def _mla_flash_kernel(q_ref, k_ref, v_ref, o_ref, m_ref, l_ref, acc_ref, *, tk):
    i = pl.program_id(2)
    tq = q_ref.shape[0]
    assert tq == tk
    nt = tk // LANES
    m_ref[...] = jnp.full_like(m_ref, NEG_INF)
    l_ref[...] = jnp.zeros_like(l_ref)
    acc_ref[...] = jnp.zeros_like(acc_ref)

    def lane_tiles(x):
        return [x[:, t * LANES:(t + 1) * LANES] for t in range(nt)]

    def step(j, masked):
        start = pl.multiple_of(j * tk, tk)
        v = v_ref[pl.ds(start, tk), :]
        for h in range(2):
            s = _dot_t(q_ref[:, h * C_SLOT:(h + 1) * C_SLOT], k_ref[pl.ds(start, tk), h * C_SLOT:(h + 1) * C_SLOT])
            if masked:
                qi = lax.broadcasted_iota(jnp.int32, (tq, tk), 0)
                kj = lax.broadcasted_iota(jnp.int32, (tq, tk), 1)
                s = jnp.where(kj <= qi, s, NEG_INF)
            tile_max = functools.reduce(jnp.maximum, lane_tiles(s))
            m_old = m_ref[h]
            m_new = jnp.maximum(m_old, jnp.broadcast_to(jnp.max(tile_max, axis=-1, keepdims=True), (tq, LANES)))
            alpha = jnp.exp(m_old - m_new)
            p = jnp.exp(s - jnp.tile(m_new, (1, nt)))
            l_ref[h] = alpha * l_ref[h] + functools.reduce(jnp.add, lane_tiles(p))
            acc_ref[h] = alpha * acc_ref[h] + _dot(p.astype(BF16), v)
            m_ref[h] = m_new

    def body(j, carry):
        step(j, False)
        return carry

    lax.fori_loop(0, i, body, 0)
    step(i, True)
    lane = lax.broadcasted_iota(jnp.int32, (tq, 2 * C_VDIM), 1)
    res = [acc_ref[h] / jnp.sum(l_ref[h], axis=-1, keepdims=True) for h in range(2)]
    o_ref[...] = jnp.where(lane < C_VDIM, res[0], res[1]).astype(o_ref.dtype)


def _mla_flash(q, k, v, tq=512):
    b, s, _ = q.shape
    tq = min(tq, s)
    pairs = C_HEADS // 2
    return pl.pallas_call(
        functools.partial(_mla_flash_kernel, tk=tq),
        grid=(b, pairs, s // tq),
        in_specs=[
            pl.BlockSpec((None, tq, 2 * C_SLOT), lambda bi, p, i: (bi, i, p)),
            pl.BlockSpec((None, s, 2 * C_SLOT), lambda bi, p, i: (bi, 0, p)),
            pl.BlockSpec((None, s, 2 * C_VDIM), lambda bi, p, i: (bi, 0, p)),
        ],
        out_specs=pl.BlockSpec((None, tq, 2 * C_VDIM), lambda bi, p, i: (bi, i, p)),
        out_shape=jax.ShapeDtypeStruct((b, s, C_HEADS * C_VDIM), BF16),
        scratch_shapes=[pltpu.VMEM((2, tq, LANES), F32)] * 3,
        compiler_params=_cparams("parallel", "parallel", "arbitrary"),
        name="mla_flash",
    )(q, k, v)


def _mixer_mla(x, tabs_c, g_pre, g_post, w_in, q_norm, w_uq, kv_norm, w_ukv, w_o):
    b, s, d = x.shape
    xf = x.reshape(b * s, d)
    kpe_w = jnp.pad(w_in[:, C_Q_RANK + C_KV_RANK:], ((0, 0), (C_NOPE, C_SLOT - C_NOPE - C_ROPE)))
    w_c = jnp.concatenate([w_in[:, C_Q_RANK:C_Q_RANK + C_KV_RANK], kpe_w, w_in[:, :C_Q_RANK]], axis=1)
    c = _norm_matmul(xf, g_pre, w_c.astype(BF16), out_dtype=F32, tn=w_c.shape[1])
    pad = C_SLOT - C_NOPE - C_ROPE
    wq = jnp.pad(w_uq.reshape(C_Q_RANK, C_HEADS, C_NOPE + C_ROPE), ((0, 0), (0, 0), (0, pad)))
    wkv = w_ukv.reshape(C_KV_RANK, C_HEADS, C_NOPE + C_VDIM)
    wk = jnp.pad(wkv[:, :, :C_NOPE], ((0, 0), (0, 0), (0, C_SLOT - C_NOPE)))
    wv = wkv[:, :, C_NOPE:]
    q, k, v = _mla_proj(c, q_norm, kv_norm, wq.reshape(C_Q_RANK, -1).astype(BF16),
                        wk.reshape(C_KV_RANK, -1).astype(BF16), wv.reshape(C_KV_RANK, -1).astype(BF16), tabs_c)
    hw = C_HEADS * C_SLOT
    o = _mla_flash(q.reshape(b, s, hw), k.reshape(b, s, hw), v.reshape(b, s, C_HEADS * C_VDIM))
    return _matmul_norm_res(o.reshape(b * s, C_HEADS * C_VDIM), w_o.astype(BF16), g_post, xf).reshape(b, s, d)


def _rwkv_proj_kernel(x_ref, gpre_ref, mix_ref, wr_ref, wk_ref, wv_ref, w1_ref, w2_ref, a1_ref, a2_ref,
                      g1_ref, g2_ref, w0_ref, a0_ref,
                      r_ref, k_ref, v_ref, lw_ref, a_ref, g_ref, hbuf_ref, carry_ref):
    i = pl.program_id(1)
    tm = x_ref.shape[0]
    hn = _rms(x_ref[...]) * gpre_ref[...]
    @pl.when(i == 0)
    def _():
        hbuf_ref[0:SUBLANES, :] = jnp.zeros((SUBLANES, hn.shape[1]), F32)

    @pl.when(i > 0)
    def _():
        hbuf_ref[0:SUBLANES, :] = carry_ref[...]

    hbuf_ref[SUBLANES:SUBLANES + tm, :] = hn
    carry_ref[...] = hn[tm - SUBLANES:, :]
    xx = hbuf_ref[SUBLANES - 1:SUBLANES - 1 + tm, :] - hn
    mix = mix_ref[...]
    mixed = lambda n: (hn + xx * mix[n:n + 1]).astype(BF16)
    r_ref[...] = _dot(mixed(0), wr_ref[...])
    k_ref[...] = _dot(mixed(2), wk_ref[...])
    v_ref[...] = _dot(mixed(3), wv_ref[...])
    wl = _dot(jnp.tanh(_dot(mixed(1), w1_ref[...])).astype(BF16), w2_ref[...])
    w = -_softplus(-(w0_ref[...] + wl)) - 0.5
    lw_ref[...] = -jnp.exp(w)
    al = _dot(_dot(mixed(4), a1_ref[...]).astype(BF16), a2_ref[...])
    a_ref[...] = _sigmoid(a0_ref[...] + al)
    g_ref[...] = _dot(_sigmoid(_dot(mixed(5), g1_ref[...])).astype(BF16), g2_ref[...]).astype(g_ref.dtype)


def _rwkv_proj(x, gpre, mix, w_rkv, w1, w2, a1, a2, g1, g2, w0, a0, tm=256):
    b, s, d = x.shape
    tm = min(tm, s)
    const = lambda a: pl.BlockSpec(a.shape, lambda bi, i: (0,) * a.ndim)
    xspec = pl.BlockSpec((None, tm, d), lambda bi, i: (bi, i, 0))
    bf = lambda a: a.astype(BF16)
    args = [gpre.reshape(1, d), mix, bf(w_rkv[0]), bf(w_rkv[1]), bf(w_rkv[2]), bf(w1), bf(w2), bf(a1), bf(a2),
            bf(g1), bf(g2), w0.reshape(1, d), a0.reshape(1, d)]
    f32_out = jax.ShapeDtypeStruct(x.shape, F32)
    return pl.pallas_call(
        _rwkv_proj_kernel,
        grid=(b, s // tm),
        in_specs=[xspec] + [const(a) for a in args],
        out_specs=[xspec] * 6,
        out_shape=[f32_out] * 5 + [jax.ShapeDtypeStruct(x.shape, BF16)],
        scratch_shapes=[pltpu.VMEM((tm + SUBLANES, d), F32), pltpu.VMEM((SUBLANES, d), F32)],
        compiler_params=_cparams("arbitrary", "arbitrary"),
        name="rwkv_proj",
    )(x, *args)


def _split(x):
    hi = x.astype(BF16)
    return hi, (x - hi.astype(F32)).astype(BF16)


_BMM_DIMS = {
    "nn": (((2,), (1,)), ((0,), (0,))),
    "nt": (((2,), (2,)), ((0,), (0,))),
}


def _bmm(a, b, passes, form="nn"):
    dims = _BMM_DIMS[form]
    dg = functools.partial(lax.dot_general, dimension_numbers=dims, preferred_element_type=F32)
    if passes == 6:
        return dg(a, b, precision=HIGHEST)
    if passes == 1:
        return dg(a.astype(BF16), b.astype(BF16))
    a_hi, a_lo = _split(a)
    b_hi, b_lo = _split(b)
    return dg(a_hi, b_hi) + (dg(a_hi, b_lo) + dg(a_lo, b_hi))


def _bt(x):
    return jnp.swapaxes(x, 1, 2)


RWKV_PASSES = dict(score=3, inv=3, apply=1, state=1, seq=1)


def _rwkv_rec_kernel(r_ref, k_ref, v_ref, lw_ref, a_ref, kk_ref, ka_ref, rk_ref, lnw_ref, lnb_ref,
                     o_ref, st_ref):
    i = pl.program_id(2)
    tc = r_ref.shape[0]
    ln, dh = D_CHUNK, D_HEAD
    nc, nh = tc // ln, LANES // D_HEAD
    ps = RWKV_PASSES

    @pl.when(i == 0)
    def _():
        st_ref[...] = jnp.zeros_like(st_ref)

    trow = lax.broadcasted_iota(jnp.int32, (tc, tc), 0)
    tcol = lax.broadcasted_iota(jnp.int32, (tc, tc), 1)
    tri = ((trow // ln == tcol // ln) & (tcol <= trow)).astype(BF16)
    lw = lw_ref[...]
    hi = lw.astype(BF16)
    rem = lw - hi.astype(F32)
    mid = rem.astype(BF16)
    lo = (rem - mid.astype(F32)).astype(BF16)
    g3 = _dot(tri, jnp.concatenate([hi, mid, lo], axis=1))
    gc = g3[:, :LANES] + (g3[:, LANES:2 * LANES] + g3[:, 2 * LANES:])
    g_end = jnp.concatenate(
        [jnp.broadcast_to(gc[(c + 1) * ln - 1:(c + 1) * ln], (ln, LANES)) for c in range(nc)], axis=0)

    lane = lax.broadcasted_iota(jnp.int32, (tc, LANES), 1)
    first = lane < dh

    def head_sum(x):
        s0 = jnp.sum(jnp.where(first, x, 0.0), axis=-1, keepdims=True)
        s1 = jnp.sum(jnp.where(first, 0.0, x), axis=-1, keepdims=True)
        return jnp.where(first, s0, s1)

    r, k, v, a = r_ref[...], k_ref[...], v_ref[...], a_ref[...]
    kk = k * kk_ref[...]
    kk = kk / jnp.maximum(jnp.sqrt(head_sum(kk * kk)), 1e-12)
    k_mod = k * (1.0 + (a - 1.0) * ka_ref[...])
    beta = kk * a
    e_neg = jnp.exp(-gc)
    e_end = jnp.exp(g_end - gc)
    a_t = -kk * jnp.exp(gc - lw)
    r_t = r * jnp.exp(gc)

    def heads(x):
        return jnp.stack([x[c * ln:(c + 1) * ln, h * dh:(h + 1) * dh] for c in range(nc) for h in range(nh)])

    a_b, r_b, v_b = heads(a_t), heads(r_t), heads(v)
    bend_b, kend_b = heads(beta * e_end), heads(k_mod * e_end)
    sc = _bmm(jnp.concatenate([a_b, r_b], axis=1),
              jnp.concatenate([heads(beta * e_neg), heads(k_mod * e_neg)], axis=1), ps["score"], "nt")
    row = lax.broadcasted_iota(jnp.int32, (ln, ln), 0)
    col = lax.broadcasted_iota(jnp.int32, (ln, ln), 1)
    incl, strict = col <= row, col < row
    eye = (row == col).astype(F32)
    n_ab = jnp.where(strict, sc[:, :ln, :ln], 0.0)
    a_ak = jnp.where(strict, sc[:, :ln, ln:], 0.0)
    a_rb = jnp.where(incl, sc[:, ln:, :ln], 0.0)
    a_rk = jnp.where(incl, sc[:, ln:, ln:], 0.0)
    n_d = jnp.where((row // D_SUB) == (col // D_SUB), n_ab, 0.0)
    n_o = n_ab - n_d
    p = eye + n_d
    pw = n_d
    for _ in range(int(math.log2(D_SUB)) - 1):
        pw = _bmm(pw, pw, ps["inv"])
        p = p + _bmm(p, pw, ps["inv"])
    m1 = _bmm(p, n_o, ps["inv"])
    m2 = _bmm(m1, m1, ps["inv"])
    assert D_CHUNK // D_SUB == 4
    av = _bmm(a_ak, v_b, ps["apply"])
    w12 = _bmm(p, jnp.concatenate([a_b, av], axis=2), ps["apply"])
    w12 = w12 + _bmm(m1, w12, ps["apply"])
    w12 = w12 + _bmm(m2, w12, ps["apply"])
    qy = _bmm(a_rb, w12, ps["apply"])
    q_eff = r_b + qy[:, :, :dh]
    y0 = qy[:, :, dh:] + _bmm(a_rk, v_b, ps["apply"])
    w1, w2 = w12[:, :, :dh], w12[:, :, dh:]
    decay = jnp.exp(heads(gc)[:, ln - 1:ln, :])
    trans = eye * decay + _bmm(_bt(w1), bend_b, ps["state"])
    s_add = _bmm(_bt(jnp.concatenate([w2, v_b], axis=1)), jnp.concatenate([bend_b, kend_b], axis=1), ps["state"])

    ys = [None] * (nc * nh)
    for h in range(nh):
        st = st_ref[h][None]
        for c in range(nc):
            b = c * nh + h
            ys[b] = _bmm(q_eff[b:b + 1], st, ps["seq"], "nt")[0] + y0[b]
            st = _bmm(st, trans[b:b + 1], ps["seq"]) + s_add[b:b + 1]
        st_ref[h] = st[0]
    y = jnp.concatenate([jnp.concatenate(ys[c * nh:(c + 1) * nh], axis=1) for c in range(nc)], axis=0)

    mu = head_sum(y) * (1.0 / dh)
    yc = y - mu
    var = head_sum(yc * yc) * (1.0 / dh)
    gn = yc * lax.rsqrt(var + D_GN_EPS) * lnw_ref[...] + lnb_ref[...]
    bonus = head_sum(r * k_mod * rk_ref[...]) * v
    o_ref[...] = gn + bonus


def _rwkv_rec(r, k, v, lw, a, k_k, k_a, r_k, lnx_w, lnx_b, tc=256):
    b, s, d = r.shape
    tc = min(tc, s)
    blk = pl.BlockSpec((None, tc, LANES), lambda bi, p, i: (bi, i, p))
    par = pl.BlockSpec((1, LANES), lambda bi, p, i: (0, p))
    params = [t.reshape(1, d) for t in (k_k, k_a, r_k, lnx_w, lnx_b)]
    return pl.pallas_call(
        _rwkv_rec_kernel,
        grid=(b, d // LANES, s // tc),
        in_specs=[blk] * 5 + [par] * 5,
        out_specs=blk,
        out_shape=jax.ShapeDtypeStruct(r.shape, F32),
        scratch_shapes=[pltpu.VMEM((LANES // D_HEAD, D_HEAD, D_HEAD), F32)],
        compiler_params=_cparams("parallel", "parallel", "arbitrary"),
        name="rwkv_rec",
    )(r, k, v, lw, a, *params)


def _mixer_rwkv7(x, g_pre, g_post, mix, w_rkv, w0, w1, w2, a0, a1, a2, g1, g2, k_k, k_a, r_k, lnx_w, lnx_b, w_o):
    b, s, d = x.shape
    r, k, v, lw, a, g = _rwkv_proj(x, g_pre, mix, w_rkv, w1, w2, a1, a2, g1, g2, w0, a0)
    y = _rwkv_rec(r, k, v, lw, a, k_k, k_a, r_k, lnx_w, lnx_b)
    flat = lambda t: t.reshape(b * s, d)
    return _matmul_norm_res(flat(y), w_o.astype(BF16), g_post, flat(x), gate=flat(g)).reshape(b, s, d)


def kernel(x, mem, positions, ln_gains, mem_norm, mem_w_kv, mem_w_q, mem_w_o, ffn_w_in, ffn_conv_w, ffn_conv_b, ffn_w_out, a_w_qkv, a_w_o, b_w_in, b_w_gate2, b_gate_bias, b_o_norm, b_w_o, c_w_in, c_q_norm, c_w_uq, c_kv_norm, c_w_ukv, c_w_o, d_mix, d_w_rkv, d_w0, d_w1, d_w2, d_a0, d_a1, d_a2, d_g1, d_g2, d_k_k, d_k_a, d_r_k, d_lnx_w, d_lnx_b, d_w_o):
    b, s, d = x.shape
    depth = ln_gains.shape[0]
    mlen = mem.shape[1]
    mwidth = M_HEADS * M_HEAD_DIM
    mkv = _norm_matmul(mem.reshape(b * mlen, d), mem_norm, mem_w_kv.astype(BF16), out_dtype=BF16,
                       tn=2 * mwidth, tm=mlen)
    mkv = mkv.reshape(b, mlen, 2 * mwidth)
    mem_k, mem_v = mkv[:, :, :mwidth], mkv[:, :, mwidth:]
    tabs = _rope_tables(positions)
    tabs_a, tabs_c = tabs[:3], tabs[3:]
    for i in range(depth):
        m, j = i % N_MIXERS, i // N_MIXERS
        gains = ln_gains[i]
        if m == 0:
            x = _mixer_dilated(x, tabs_a, gains[0], gains[1], a_w_qkv[j], a_w_o[j])
        elif m == 1:
            x = _mixer_gla(x, gains[0], gains[1], b_w_in[j], b_w_gate2[j], b_gate_bias[j], b_o_norm[j], b_w_o[j])
        elif m == 2:
            x = _mixer_mla(x, tabs_c, gains[0], gains[1], c_w_in[j], c_q_norm[j], c_w_uq[j], c_kv_norm[j],
                           c_w_ukv[j], c_w_o[j])
        else:
            x = _mixer_rwkv7(x, gains[0], gains[1], d_mix[j], d_w_rkv[j], d_w0[j], d_w1[j], d_w2[j], d_a0[j],
                             d_a1[j], d_a2[j], d_g1[j], d_g2[j], d_k_k[j], d_k_a[j], d_r_k[j], d_lnx_w[j],
                             d_lnx_b[j], d_w_o[j])
        x = _mem_attn(x, gains[2], mem_w_q[i].astype(BF16), mem_k, mem_v, mem_w_o[i].astype(BF16), gains[3])
        x = _conv_ffn(x, gains[4], ffn_w_in[i], ffn_conv_w[i], ffn_conv_b[i], ffn_w_out[i], gains[5])
    return x
```

```python
import functools
import math

import jax
import jax.numpy as jnp
from jax import lax
from jax.experimental import pallas as pl
from jax.experimental.pallas import tpu as pltpu

F32 = jnp.float32
BF16 = jnp.bfloat16
HIGHEST = lax.Precision.HIGHEST

D_MODEL = 1024
N_MIXERS = 4
NORM_EPS = 1e-6
ROPE_THETA = 500000.0
NEG_INF = -1e30

A_HEAD_DIM = 64
A_HEADS = 8
A_GROUPS = ((128, 1), (512, 4), (2048, 16))
A_BLOCK = 128
A_ROT = A_HEAD_DIM // 4
A_WIDTH = A_HEADS * A_HEAD_DIM

B_HEADS = 4
B_KEY_DIM = 512
B_VAL_DIM = 1024
B_DK = B_KEY_DIM // B_HEADS
B_DV = B_VAL_DIM // B_HEADS
B_GATE_RANK = 16
B_GATE_NORMALIZER = 16.0
B_CHUNK = 64

C_HEADS = 16
C_Q_RANK = 384
C_KV_RANK = 256
C_NOPE = 64
C_ROPE = 32
C_VDIM = 64
C_SLOT = 128

D_HEAD = 64
D_HEADS = D_MODEL // D_HEAD
D_GN_EPS = 64e-5
D_CHUNK = 64
D_SUB = 16

M_HEADS = 4
M_HEAD_DIM = 128

D_FF = 2816
CONV_WIDTH = 3
FF_CHUNK = 256

LANES = 128
SUBLANES = 8
VMEM_LIMIT = 56 * 1024 * 1024


def _cparams(*sem):
    return pltpu.CompilerParams(dimension_semantics=sem, vmem_limit_bytes=VMEM_LIMIT)


def _rms(x, eps=NORM_EPS):
    return x * lax.rsqrt(jnp.mean(x * x, axis=-1, keepdims=True) + eps)


def _sigmoid(x):
    return 1.0 / (1.0 + jnp.exp(-x))


def _softplus(x):
    return jnp.maximum(x, 0.0) + jnp.log(1.0 + jnp.exp(-jnp.abs(x)))


def _dot(a, b, **kw):
    return jnp.dot(a, b, preferred_element_type=F32, **kw)


def _dot_t(a, b, **kw):
    return lax.dot_general(a, b, (((1,), (1,)), ((), ())), preferred_element_type=F32, **kw)


def _tdot(a, b, **kw):
    return lax.dot_general(a, b, (((0,), (0,)), ((), ())), preferred_element_type=F32, **kw)


def _rope_tile(y, c, s1, s2, shift):
    return y * c + pltpu.roll(y, LANES - shift, 1) * s1 + pltpu.roll(y, shift, 1) * s2


def _rope_table_kernel(pos_ref, ca_ref, sa1_ref, sa2_ref, cc_ref, sc1_ref, sc2_ref):
    pos = pos_ref[...].astype(F32)
    lane = lax.broadcasted_iota(jnp.int32, (1, LANES), 1)
    log_theta = math.log(ROPE_THETA)

    def tables(rel, half, rot):
        in_span = (rel >= 0) & (rel < rot)
        idx = jnp.where(in_span, jnp.where(rel < half, rel, rel - half), 0).astype(F32)
        inv_freq = jnp.exp(-(idx * (2.0 / rot)) * log_theta)
        ang = pos * inv_freq
        cos, sin = jnp.cos(ang), jnp.sin(ang)
        first = (rel >= 0) & (rel < half)
        second = (rel >= half) & (rel < rot)
        return (jnp.where(in_span, cos, 1.0), jnp.where(first, -sin, 0.0), jnp.where(second, sin, 0.0))

    ca, sa1, sa2 = tables(lane % A_HEAD_DIM, A_ROT // 2, A_ROT)
    cc, sc1, sc2 = tables(lane - C_NOPE, C_ROPE // 2, C_ROPE)
    ca_ref[...], sa1_ref[...], sa2_ref[...] = ca, sa1, sa2
    cc_ref[...], sc1_ref[...], sc2_ref[...] = cc, sc1, sc2


def _rope_tables(positions):
    t = positions.size
    tm = min(t, 1024)
    pos = positions.reshape(t, 1)
    out = jax.ShapeDtypeStruct((t, LANES), F32)
    spec = pl.BlockSpec((tm, LANES), lambda i: (i, 0))
    return pl.pallas_call(
        _rope_table_kernel,
        grid=(t // tm,),
        in_specs=[pl.BlockSpec((tm, 1), lambda i: (i, 0))],
        out_specs=[spec] * 6,
        out_shape=[out] * 6,
        compiler_params=_cparams("parallel"),
        name="rope_tables",
    )(pos)


def _norm_matmul_kernel(x_ref, g_ref, w_ref, o_ref, xn_ref):
    @pl.when(pl.program_id(1) == 0)
    def _():
        xn_ref[...] = (_rms(x_ref[...]) * g_ref[...]).astype(BF16)

    o_ref[...] = _dot(xn_ref[...], w_ref[...]).astype(o_ref.dtype)


def _norm_matmul(x, g, w, *, out_dtype, tn, tm=1024):
    t, k = x.shape
    n = w.shape[1]
    tm = min(tm, t)
    return pl.pallas_call(
        _norm_matmul_kernel,
        grid=(t // tm, n // tn),
        in_specs=[
            pl.BlockSpec((tm, k), lambda i, j: (i, 0)),
            pl.BlockSpec((1, k), lambda i, j: (0, 0)),
            pl.BlockSpec((k, tn), lambda i, j: (0, j)),
        ],
        out_specs=pl.BlockSpec((tm, tn), lambda i, j: (i, j)),
        out_shape=jax.ShapeDtypeStruct((t, n), out_dtype),
        scratch_shapes=[pltpu.VMEM((tm, k), BF16)],
        compiler_params=_cparams("parallel", "arbitrary"),
        name="norm_matmul",
    )(x, g.reshape(1, k), w)


def _matmul_norm_res_kernel(*refs, gated):
    if gated:
        h_ref, gate_ref, w_ref, g_ref, res_ref, o_ref = refs
        h = (h_ref[...] * gate_ref[...].astype(F32)).astype(BF16)
    else:
        h_ref, w_ref, g_ref, res_ref, o_ref = refs
        h = h_ref[...]
    y = _dot(h, w_ref[...])
    o_ref[...] = res_ref[...] + _rms(y) * g_ref[...]


def _matmul_norm_res(h, w, g, res, gate=None, tm=512):
    t, k = h.shape
    n = w.shape[1]
    tm = min(tm, t)
    row = lambda width: pl.BlockSpec((tm, width), lambda i: (i, 0))
    full = lambda a: pl.BlockSpec(a.shape, lambda i: (0, 0))
    g2 = g.reshape(1, n)
    if gate is None:
        args, specs = [h, w, g2, res], [row(k), full(w), full(g2), row(n)]
    else:
        args, specs = [h, gate, w, g2, res], [row(k), row(k), full(w), full(g2), row(n)]
    return pl.pallas_call(
        functools.partial(_matmul_norm_res_kernel, gated=gate is not None),
        grid=(t // tm,),
        in_specs=specs,
        out_specs=row(n),
        out_shape=jax.ShapeDtypeStruct((t, n), F32),
        compiler_params=_cparams("parallel"),
        name="matmul_norm_res",
    )(*args)


def _mem_attn_kernel(x_ref, gpre_ref, wq_ref, k_ref, v_ref, wo_ref, gpost_ref, o_ref):
    x = x_ref[...]
    hn = (_rms(x) * gpre_ref[...]).astype(BF16)
    q = (_dot(hn, wq_ref[...]) * (M_HEAD_DIM ** -0.5)).astype(BF16)
    k, v = k_ref[...], v_ref[...]
    outs = []
    for h in range(M_HEADS):
        sl = slice(h * M_HEAD_DIM, (h + 1) * M_HEAD_DIM)
        s = _dot_t(q[:, sl], k[:, sl])
        p = jnp.exp(s - jnp.max(s, axis=-1, keepdims=True))
        p = p / jnp.sum(p, axis=-1, keepdims=True)
        outs.append(_dot(p.astype(BF16), v[:, sl]))
    o = jnp.concatenate(outs, axis=1).astype(BF16)
    y = _dot(o, wo_ref[...])
    o_ref[...] = x + _rms(y) * gpost_ref[...]


def _mem_attn(x, gpre, wq, mem_k, mem_v, wo, gpost, tm=512):
    b, s, d = x.shape
    tm = min(tm, s)
    width = M_HEADS * M_HEAD_DIM
    mlen = mem_k.shape[1]
    const = lambda a: pl.BlockSpec(a.shape, lambda bi, i: (0,) * a.ndim)
    gpre, gpost = gpre.reshape(1, d), gpost.reshape(1, d)
    xspec = pl.BlockSpec((None, tm, d), lambda bi, i: (bi, i, 0))
    mspec = pl.BlockSpec((None, mlen, width), lambda bi, i: (bi, 0, 0))
    return pl.pallas_call(
        _mem_attn_kernel,
        grid=(b, s // tm),
        in_specs=[xspec, const(gpre), const(wq), mspec, mspec, const(wo), const(gpost)],
        out_specs=xspec,
        out_shape=jax.ShapeDtypeStruct(x.shape, F32),
        compiler_params=_cparams("parallel", "parallel"),
        name="mem_attn",
    )(x, gpre, wq, mem_k, mem_v, wo, gpost)


def _conv_ffn_kernel(x_ref, gpre_ref, win_ref, cw_ref, cb_ref, wout_ref, gpost_ref, o_ref, carry_ref):
    i = pl.program_id(1)
    tm = x_ref.shape[0]
    fc = FF_CHUNK

    @pl.when(i == 0)
    def _():
        carry_ref[...] = jnp.zeros_like(carry_ref)

    x = x_ref[...]
    hn = (_rms(x) * gpre_ref[...]).astype(BF16)
    acc = jnp.zeros((tm, x.shape[1]), F32)
    for j in range(D_FF // fc):
        cols = slice(2 * j * fc, 2 * (j + 1) * fc)
        u = _dot(hn, win_ref[:, cols])
        ext = jnp.concatenate([carry_ref[:, cols], u], axis=0)
        carry_ref[:, cols] = u[tm - SUBLANES:, :]
        cw = cw_ref[:, cols]
        conv = (cb_ref[:, cols] + ext[SUBLANES - 2:SUBLANES - 2 + tm] * cw[0:1]
                + ext[SUBLANES - 1:SUBLANES - 1 + tm] * cw[1:2] + u * cw[2:3])
        gate, val = conv[:, :fc], conv[:, fc:]
        h = (gate * _sigmoid(gate) * val).astype(BF16)
        acc = acc + _dot(h, wout_ref[j * fc:(j + 1) * fc, :])
    o_ref[...] = x + _rms(acc) * gpost_ref[...]


def _conv_ffn(x, gpre, w_in, conv_w, conv_b, w_out, gpost, tm=512):
    b, s, d = x.shape
    tm = min(tm, s)
    fc = FF_CHUNK
    nc = D_FF // fc

    def interleave(a):
        lead = a.shape[:-1]
        a = a.reshape(lead + (2, nc, fc))
        return jnp.swapaxes(a, -3, -2).reshape(lead + (2 * D_FF,))

    w_in = interleave(w_in).astype(BF16)
    conv_w = interleave(conv_w)
    conv_b = interleave(conv_b).reshape(1, 2 * D_FF)
    w_out = w_out.astype(BF16)
    gpre, gpost = gpre.reshape(1, d), gpost.reshape(1, d)
    xspec = pl.BlockSpec((None, tm, d), lambda bi, i: (bi, i, 0))
    const = lambda a: pl.BlockSpec(a.shape, lambda bi, i: (0, 0), pipeline_mode=pl.Buffered(1))
    return pl.pallas_call(
        _conv_ffn_kernel,
        grid=(b, s // tm),
        in_specs=[xspec, const(gpre), const(w_in), const(conv_w), const(conv_b), const(w_out), const(gpost)],
        out_specs=xspec,
        out_shape=jax.ShapeDtypeStruct(x.shape, F32),
        scratch_shapes=[pltpu.VMEM((SUBLANES, 2 * D_FF), F32)],
        compiler_params=_cparams("arbitrary", "arbitrary"),
        name="conv_ffn",
    )(x, gpre, w_in, conv_w, conv_b, w_out, gpost)


def _dilated_qkv_kernel(x_ref, g_ref, w_ref, c_ref, s1_ref, s2_ref, o_ref, xn_ref, ybuf_ref, *, dil):
    sec = pl.program_id(2)
    tm = x_ref.shape[0]

    @pl.when(sec == 0)
    def _():
        xn_ref[...] = (_rms(x_ref[...]) * g_ref[...]).astype(BF16)

    y = _dot(xn_ref[...], w_ref[...])
    on = sec < 2
    c = jnp.where(on, c_ref[...], 1.0)
    s1 = jnp.where(on, s1_ref[...], 0.0)
    s2 = jnp.where(on, s2_ref[...], 0.0)
    scale = jnp.where(sec == 0, A_HEAD_DIM ** -0.5, 1.0)
    y = jnp.concatenate([_rope_tile(y[:, t * LANES:(t + 1) * LANES], c, s1, s2, A_ROT // 2)
                         for t in range(A_WIDTH // LANES)], axis=1) * scale
    if dil == 1:
        o_ref[0] = y.astype(BF16)
    else:
        nt = A_WIDTH // LANES
        for t in range(nt):
            ybuf_ref[t] = y[:, t * LANES:(t + 1) * LANES]
        for r in range(dil):
            o_ref[r] = jnp.concatenate(
                [ybuf_ref[t, pl.ds(r, tm // dil, stride=dil), :] for t in range(nt)], axis=1).astype(BF16)


def _dilated_qkv(x, g_pre, w, tabs, dil, tm=1024):
    b, s, d = x.shape
    tm = min(tm, s)
    nblk = s // tm
    tab = pl.BlockSpec((tm, LANES), lambda bi, i, sec: (bi * nblk + i, 0))
    return pl.pallas_call(
        functools.partial(_dilated_qkv_kernel, dil=dil),
        grid=(b, nblk, 3),
        in_specs=[
            pl.BlockSpec((None, tm, d), lambda bi, i, sec: (bi, i, 0)),
            pl.BlockSpec((1, d), lambda bi, i, sec: (0, 0)),
            pl.BlockSpec((d, A_WIDTH), lambda bi, i, sec: (0, sec)),
            tab, tab, tab,
        ],
        out_specs=pl.BlockSpec((None, dil, tm // dil, A_WIDTH), lambda bi, i, sec: (bi, 0, i, sec)),
        out_shape=jax.ShapeDtypeStruct((b, dil, s // dil, 3 * A_WIDTH), BF16),
        scratch_shapes=[pltpu.VMEM((tm, d), BF16), pltpu.VMEM((A_WIDTH // LANES, tm, LANES), F32)],
        compiler_params=_cparams("parallel", "parallel", "arbitrary"),
        name="dilated_qkv",
    )(x, g_pre.reshape(1, d), w, *tabs)


def _dilated_attn_kernel(q_ref, kc_ref, kp_ref, vc_ref, vp_ref, o_ref, lse_ref):
    n = pl.program_id(2)
    qi = lax.broadcasted_iota(jnp.int32, (A_BLOCK, A_BLOCK), 0)
    kj = lax.broadcasted_iota(jnp.int32, (A_BLOCK, A_BLOCK), 1)
    mask_p = (kj >= qi) & (n > 0)
    mask_c = kj <= qi
    first = lax.broadcasted_iota(jnp.int32, (A_BLOCK, LANES), 1) < A_HEAD_DIM
    zero = jnp.zeros((), BF16)
    for hp in range(A_WIDTH // LANES):
        sl = slice(hp * LANES, (hp + 1) * LANES)
        q, kc, kp, vc, vp = q_ref[:, sl], kc_ref[:, sl], kp_ref[:, sl], vc_ref[:, sl], vp_ref[:, sl]
        o_pair = lse_pair = None
        for h in range(2):
            sel = first if h == 0 else jnp.logical_not(first)
            qh = jnp.where(sel, q, zero)
            s_p = jnp.where(mask_p, _dot_t(qh, kp), NEG_INF)
            s_c = jnp.where(mask_c, _dot_t(qh, kc), NEG_INF)
            m = jnp.maximum(jnp.max(s_p, axis=-1, keepdims=True), jnp.max(s_c, axis=-1, keepdims=True))
            p_p, p_c = jnp.exp(s_p - m), jnp.exp(s_c - m)
            l = jnp.sum(p_p, axis=-1, keepdims=True) + jnp.sum(p_c, axis=-1, keepdims=True)
            o = (_dot(p_p.astype(BF16), jnp.where(sel, vp, zero)) + _dot(p_c.astype(BF16), jnp.where(sel, vc, zero))) / l
            lse = jnp.broadcast_to(m + jnp.log(l), (A_BLOCK, LANES))
            o_pair = o if h == 0 else o_pair + o
            lse_pair = lse if h == 0 else jnp.where(first, lse_pair, lse)
        o_ref[:, sl] = o_pair.astype(o_ref.dtype)
        lse_ref[:, sl] = lse_pair


def _dilated_attn(qkv):
    b, dil, ln, _ = qkv.shape
    nb = ln // A_BLOCK

    def spec(section, prev):
        def index(bi, r, n):
            return (bi, r, jnp.maximum(n - 1, 0) if prev else n, section)
        return pl.BlockSpec((None, None, A_BLOCK, A_WIDTH), index)

    ospec = pl.BlockSpec((None, None, A_BLOCK, A_WIDTH), lambda bi, r, n: (bi, r, n, 0))
    return pl.pallas_call(
        _dilated_attn_kernel,
        grid=(b, dil, nb),
        in_specs=[spec(0, False), spec(1, False), spec(1, True), spec(2, False), spec(2, True)],
        out_specs=[ospec, ospec],
        out_shape=[jax.ShapeDtypeStruct((b, dil, ln, A_WIDTH), BF16),
                   jax.ShapeDtypeStruct((b, dil, ln, A_WIDTH), F32)],
        compiler_params=_cparams("parallel", "parallel", "arbitrary"),
        name="dilated_attn",
    )(qkv, qkv, qkv, qkv, qkv)


def _dilated_combine_kernel(*refs, dils):
    ng = len(dils)
    o_refs, l_refs = refs[:ng], refs[ng:2 * ng]
    w_ref, g_ref, res_ref, out_ref = refs[2 * ng:2 * ng + 4]
    bufs = list(refs[2 * ng + 4:])
    tm = res_ref.shape[0]

    def natural(ref, dil):
        if dil == 1:
            return ref[0].astype(F32)
        buf = bufs.pop()
        nt = A_WIDTH // LANES
        for r in range(dil):
            rows = ref[r].astype(F32)
            for t in range(nt):
                buf[t, pl.ds(r, tm // dil, stride=dil), :] = rows[:, t * LANES:(t + 1) * LANES]
        return jnp.concatenate([buf[t] for t in range(nt)], axis=1)

    os_ = [natural(r, d) for r, d in zip(o_refs, dils)]
    ls = [natural(r, d) for r, d in zip(l_refs, dils)]
    m = functools.reduce(jnp.maximum, ls)
    es = [jnp.exp(l - m) for l in ls]
    o = sum(e * o for e, o in zip(es, os_)) / sum(es)
    y = _dot(o.astype(BF16), w_ref[...])
    out_ref[...] = res_ref[...] + _rms(y) * g_ref[...]


def _dilated_combine(os_, lses, w_o, g, res, tm=512):
    b, s, d = res.shape
    tm = min(tm, s)
    dils = tuple(o.shape[1] for o in os_)
    cls = lambda dil: pl.BlockSpec((None, dil, tm // dil, A_WIDTH), lambda bi, i: (bi, 0, i, 0))
    xspec = pl.BlockSpec((None, tm, d), lambda bi, i: (bi, i, 0))
    n_buf = 2 * sum(dil > 1 for dil in dils)
    return pl.pallas_call(
        functools.partial(_dilated_combine_kernel, dils=dils),
        grid=(b, s // tm),
        in_specs=[cls(dil) for dil in dils] * 2 + [pl.BlockSpec(w_o.shape, lambda bi, i: (0, 0)),
                                                   pl.BlockSpec((1, d), lambda bi, i: (0, 0)), xspec],
        out_specs=xspec,
        out_shape=jax.ShapeDtypeStruct(res.shape, F32),
        scratch_shapes=[pltpu.VMEM((A_WIDTH // LANES, tm, LANES), F32)] * n_buf,
        compiler_params=_cparams("parallel", "parallel"),
        name="dilated_combine",
    )(*os_, *lses, w_o, g.reshape(1, d), res)


def _mixer_dilated(x, tabs_a, g_pre, g_post, w_qkv, w_o):
    ngrp = len(A_GROUPS)
    w = w_qkv.astype(BF16).reshape(w_qkv.shape[0], 3, ngrp, A_WIDTH)
    os_, lses = [], []
    for g, (_, dil) in enumerate(A_GROUPS):
        qkv = _dilated_qkv(x, g_pre, w[:, :, g].reshape(-1, 3 * A_WIDTH), tabs_a, dil)
        o, lse = _dilated_attn(qkv)
        os_.append(o)
        lses.append(lse)
    return _dilated_combine(os_, lses, w_o.astype(BF16), g_post, x)


def _gla_kernel(q_ref, k_ref, v_ref, go_ref, glr_ref, w2_ref, bg_ref, on_ref, o_ref, st_ref):
    i = pl.program_id(2)
    tc = q_ref.shape[0]
    c_len = B_CHUNK

    @pl.when(i == 0)
    def _():
        st_ref[...] = jnp.zeros_like(st_ref)

    z = _dot(glr_ref[...].astype(BF16), w2_ref[...]) + bg_ref[...]
    log_a = -_softplus(-z) * (1.0 / B_GATE_NORMALIZER)
    row = lax.broadcasted_iota(jnp.int32, (c_len, c_len), 0)
    col = lax.broadcasted_iota(jnp.int32, (c_len, c_len), 1)
    causal = col <= row
    tri = causal.astype(F32)
    scale = B_DK ** -0.5
    for c in range(tc // c_len):
        rows = slice(c * c_len, (c + 1) * c_len)
        gcum = _dot(tri, log_a[rows], precision=HIGHEST)
        g_end = gcum[c_len - 1:c_len]
        q = q_ref[rows, :].astype(F32)
        k = k_ref[rows, :].astype(F32)
        v = v_ref[rows, :]
        q_dec = (q * scale * jnp.exp(gcum)).astype(BF16)
        k_inv = (k * jnp.exp(-gcum)).astype(BF16)
        k_end = (k * jnp.exp(g_end - gcum)).astype(BF16)
        att = jnp.where(causal, _dot_t(q_dec, k_inv), 0.0).astype(BF16)
        st = st_ref[...]
        o = _dot(att, v) + _dot_t(q_dec, st.astype(BF16))
        st_ref[...] = st * jnp.exp(g_end) + _tdot(v, k_end)
        go = go_ref[rows, :].astype(F32)
        o = _rms(o) * on_ref[...] * (go * _sigmoid(go))
        o_ref[rows, :] = o.astype(o_ref.dtype)


def _gla(proj, glr, w_gate2, b_gate, o_norm, tc=256):
    b, s, _ = proj.shape
    tc = min(tc, s)
    nkb = B_KEY_DIM // B_DK
    nvb = 2 * B_KEY_DIM // B_DV
    ngb = nvb + B_VAL_DIM // B_DV
    return pl.pallas_call(
        _gla_kernel,
        grid=(b, B_HEADS, s // tc),
        in_specs=[
            pl.BlockSpec((None, tc, B_DK), lambda bi, h, i: (bi, i, h)),
            pl.BlockSpec((None, tc, B_DK), lambda bi, h, i: (bi, i, nkb + h)),
            pl.BlockSpec((None, tc, B_DV), lambda bi, h, i: (bi, i, nvb + h)),
            pl.BlockSpec((None, tc, B_DV), lambda bi, h, i: (bi, i, ngb + h)),
            pl.BlockSpec((None, tc, LANES), lambda bi, h, i: (bi, i, 0)),
            pl.BlockSpec((LANES, B_DK), lambda bi, h, i: (0, h)),
            pl.BlockSpec((1, B_DK), lambda bi, h, i: (0, h)),
            pl.BlockSpec((1, B_DV), lambda bi, h, i: (0, 0)),
        ],
        out_specs=pl.BlockSpec((None, tc, B_DV), lambda bi, h, i: (bi, i, h)),
        out_shape=jax.ShapeDtypeStruct((b, s, B_VAL_DIM), BF16),
        scratch_shapes=[pltpu.VMEM((B_DV, B_DK), F32)],
        compiler_params=_cparams("parallel", "parallel", "arbitrary"),
        name="gla",
    )(proj, proj, proj, proj, glr, w_gate2, b_gate, o_norm)


def _mixer_gla(x, g_pre, g_post, w_in, w_gate2, b_gate, o_norm, w_o):
    b, s, d = x.shape
    xf = x.reshape(b * s, d)
    main = 2 * B_KEY_DIM + 2 * B_VAL_DIM
    w_main = w_in[:, :main].astype(BF16)
    w_lr = jnp.pad(w_in[:, main:], ((0, 0), (0, LANES - B_GATE_RANK))).astype(BF16)
    proj = _norm_matmul(xf, g_pre, w_main, out_dtype=BF16, tn=main // 4)
    glr = _norm_matmul(xf, g_pre, w_lr, out_dtype=F32, tn=LANES)
    w2 = jnp.pad(w_gate2, ((0, LANES - B_GATE_RANK), (0, 0))).astype(BF16)
    og = _gla(proj.reshape(b, s, main), glr.reshape(b, s, LANES), w2, b_gate.reshape(1, B_KEY_DIM),
              o_norm.reshape(1, B_DV))
    return _matmul_norm_res(og.reshape(b * s, B_VAL_DIM), w_o.astype(BF16), g_post, xf).reshape(b, s, d)


def _mla_proj_kernel(c_ref, qn_ref, kvn_ref, wq_ref, wk_ref, wv_ref, cc_ref, s1_ref, s2_ref,
                     q_ref, k_ref, v_ref):
    c = c_ref[...]
    ckv = c[:, :C_KV_RANK]
    kpe = c[:, C_KV_RANK:C_KV_RANK + C_SLOT]
    cq = c[:, C_KV_RANK + C_SLOT:]
    cc, s1, s2 = cc_ref[...], s1_ref[...], s2_ref[...]
    shift = C_ROPE // 2
    scale = (C_NOPE + C_ROPE) ** -0.5
    q = _dot((_rms(cq) * qn_ref[...]).astype(BF16), wq_ref[...])
    ckvn = (_rms(ckv) * kvn_ref[...]).astype(BF16)
    kn = _dot(ckvn, wk_ref[...])
    kpe = _rope_tile(kpe, cc, s1, s2, shift)
    for h in range(C_HEADS):
        sl = slice(h * C_SLOT, (h + 1) * C_SLOT)
        q_ref[:, sl] = (_rope_tile(q[:, sl], cc, s1, s2, shift) * scale).astype(BF16)
        k_ref[:, sl] = (kn[:, sl] + kpe).astype(BF16)
    v_ref[...] = _dot(ckvn, wv_ref[...]).astype(BF16)


def _mla_proj(c, q_norm, kv_norm, wq, wk, wv, tabs, tm=512):
    t = c.shape[0]
    tm = min(tm, t)
    row = lambda width: pl.BlockSpec((tm, width), lambda i: (i, 0))
    full = lambda a: pl.BlockSpec(a.shape, lambda i: (0, 0))
    qn, kvn = q_norm.reshape(1, C_Q_RANK), kv_norm.reshape(1, C_KV_RANK)
    hw = C_HEADS * C_SLOT
    return pl.pallas_call(
        _mla_proj_kernel,
        grid=(t // tm,),
        in_specs=[row(c.shape[1]), full(qn), full(kvn), full(wq), full(wk), full(wv)] + [row(LANES)] * 3,
        out_specs=[row(hw), row(hw), row(C_HEADS * C_VDIM)],
        out_shape=[jax.ShapeDtypeStruct((t, hw), BF16), jax.ShapeDtypeStruct((t, hw), BF16),
                   jax.ShapeDtypeStruct((t, C_HEADS * C_VDIM), BF16)],
        compiler_params=_cparams("parallel"),
        name="mla_proj",
    )(c, qn, kvn, wq, wk, wv, *tabs)


def _mla_flash_kernel(q_ref, k_ref, v_ref, o_ref, m_ref, l_ref, acc_ref, *, tk):
    i = pl.program_id(2)
    tq = q_ref.shape[0]
    assert tq == tk
    nt = tk // LANES
    m_ref[...] = jnp.full_like(m_ref, NEG_INF)
    l_ref[...] = jnp.zeros_like(l_ref)
    acc_ref[...] = jnp.zeros_like(acc_ref)

    def lane_tiles(x):
        return [x[:, t * LANES:(t + 1) * LANES] for t in range(nt)]

    def step(j, masked):
        start = pl.multiple_of(j * tk, tk)
        v = v_ref[pl.ds(start, tk), :]
        for h in range(2):
            s = _dot_t(q_ref[:, h * C_SLOT:(h + 1) * C_SLOT], k_ref[pl.ds(start, tk), h * C_SLOT:(h + 1) * C_SLOT])
            if masked:
                qi = lax.broadcasted_iota(jnp.int32, (tq, tk), 0)
                kj = lax.broadcasted_iota(jnp.int32, (tq, tk), 1)
                s = jnp.where(kj <= qi, s, NEG_INF)
            tile_max = functools.reduce(jnp.maximum, lane_tiles(s))
            m_old = m_ref[h]
            m_new = jnp.maximum(m_old, jnp.broadcast_to(jnp.max(tile_max, axis=-1, keepdims=True), (tq, LANES)))
            alpha = jnp.exp(m_old - m_new)
            p = jnp.exp(s - jnp.tile(m_new, (1, nt)))
            l_ref[h] = alpha * l_ref[h] + functools.reduce(jnp.add, lane_tiles(p))
            acc_ref[h] = alpha * acc_ref[h] + _dot(p.astype(BF16), v)
            m_ref[h] = m_new

    def body(jj, carry):
        step(2 * jj, False)
        step(2 * jj + 1, False)
        return carry

    lax.fori_loop(0, i // 2, body, 0)

    @pl.when(i % 2 == 1)
    def _():
        step(i - 1, False)

    step(i, True)
    lane = lax.broadcasted_iota(jnp.int32, (tq, 2 * C_VDIM), 1)
    res = [acc_ref[h] / jnp.sum(l_ref[h], axis=-1, keepdims=True) for h in range(2)]
    o_ref[...] = jnp.where(lane < C_VDIM, res[0], res[1]).astype(o_ref.dtype)


def _mla_flash(q, k, v, tq=512):
    b, s, _ = q.shape
    tq = min(tq, s)
    pairs = C_HEADS // 2
    return pl.pallas_call(
        functools.partial(_mla_flash_kernel, tk=tq),
        grid=(b, pairs, s // tq),
        in_specs=[
            pl.BlockSpec((None, tq, 2 * C_SLOT), lambda bi, p, i: (bi, i, p)),
            pl.BlockSpec((None, s, 2 * C_SLOT), lambda bi, p, i: (bi, 0, p)),
            pl.BlockSpec((None, s, 2 * C_VDIM), lambda bi, p, i: (bi, 0, p)),
        ],
        out_specs=pl.BlockSpec((None, tq, 2 * C_VDIM), lambda bi, p, i: (bi, i, p)),
        out_shape=jax.ShapeDtypeStruct((b, s, C_HEADS * C_VDIM), BF16),
        scratch_shapes=[pltpu.VMEM((2, tq, LANES), F32)] * 3,
        compiler_params=_cparams("parallel", "parallel", "arbitrary"),
        name="mla_flash",
    )(q, k, v)


def _mixer_mla(x, tabs_c, g_pre, g_post, w_in, q_norm, w_uq, kv_norm, w_ukv, w_o):
    b, s, d = x.shape
    xf = x.reshape(b * s, d)
    kpe_w = jnp.pad(w_in[:, C_Q_RANK + C_KV_RANK:], ((0, 0), (C_NOPE, C_SLOT - C_NOPE - C_ROPE)))
    w_c = jnp.concatenate([w_in[:, C_Q_RANK:C_Q_RANK + C_KV_RANK], kpe_w, w_in[:, :C_Q_RANK]], axis=1)
    c = _norm_matmul(xf, g_pre, w_c.astype(BF16), out_dtype=F32, tn=w_c.shape[1])
    pad = C_SLOT - C_NOPE - C_ROPE
    wq = jnp.pad(w_uq.reshape(C_Q_RANK, C_HEADS, C_NOPE + C_ROPE), ((0, 0), (0, 0), (0, pad)))
    wkv = w_ukv.reshape(C_KV_RANK, C_HEADS, C_NOPE + C_VDIM)
    wk = jnp.pad(wkv[:, :, :C_NOPE], ((0, 0), (0, 0), (0, C_SLOT - C_NOPE)))
    wv = wkv[:, :, C_NOPE:]
    q, k, v = _mla_proj(c, q_norm, kv_norm, wq.reshape(C_Q_RANK, -1).astype(BF16),
                        wk.reshape(C_KV_RANK, -1).astype(BF16), wv.reshape(C_KV_RANK, -1).astype(BF16), tabs_c)
    hw = C_HEADS * C_SLOT
    o = _mla_flash(q.reshape(b, s, hw), k.reshape(b, s, hw), v.reshape(b, s, C_HEADS * C_VDIM))
    return _matmul_norm_res(o.reshape(b * s, C_HEADS * C_VDIM), w_o.astype(BF16), g_post, xf).reshape(b, s, d)


def _rwkv_proj_kernel(x_ref, gpre_ref, mix_ref, wr_ref, wk_ref, wv_ref, w1_ref, w2_ref, a1_ref, a2_ref,
                      g1_ref, g2_ref, w0_ref, a0_ref,
                      r_ref, k_ref, v_ref, lw_ref, a_ref, g_ref, hbuf_ref, carry_ref):
    i = pl.program_id(1)
    tm = x_ref.shape[0]
    hn = _rms(x_ref[...]) * gpre_ref[...]
    @pl.when(i == 0)
    def _():
        hbuf_ref[0:SUBLANES, :] = jnp.zeros((SUBLANES, hn.shape[1]), F32)

    @pl.when(i > 0)
    def _():
        hbuf_ref[0:SUBLANES, :] = carry_ref[...]

    hbuf_ref[SUBLANES:SUBLANES + tm, :] = hn
    carry_ref[...] = hn[tm - SUBLANES:, :]
    xx = hbuf_ref[SUBLANES - 1:SUBLANES - 1 + tm, :] - hn
    mix = mix_ref[...]
    mixed = lambda n: (hn + xx * mix[n:n + 1]).astype(BF16)
    r_ref[...] = _dot(mixed(0), wr_ref[...])
    k_ref[...] = _dot(mixed(2), wk_ref[...])
    v_ref[...] = _dot(mixed(3), wv_ref[...])
    wl = _dot(jnp.tanh(_dot(mixed(1), w1_ref[...])).astype(BF16), w2_ref[...])
    w = -_softplus(-(w0_ref[...] + wl)) - 0.5
    lw_ref[...] = -jnp.exp(w)
    al = _dot(_dot(mixed(4), a1_ref[...]).astype(BF16), a2_ref[...])
    a_ref[...] = _sigmoid(a0_ref[...] + al)
    g_ref[...] = _dot(_sigmoid(_dot(mixed(5), g1_ref[...])).astype(BF16), g2_ref[...]).astype(g_ref.dtype)


def _rwkv_proj(x, gpre, mix, w_rkv, w1, w2, a1, a2, g1, g2, w0, a0, tm=256):
    b, s, d = x.shape
    tm = min(tm, s)
    const = lambda a: pl.BlockSpec(a.shape, lambda bi, i: (0,) * a.ndim)
    xspec = pl.BlockSpec((None, tm, d), lambda bi, i: (bi, i, 0))
    bf = lambda a: a.astype(BF16)
    args = [gpre.reshape(1, d), mix, bf(w_rkv[0]), bf(w_rkv[1]), bf(w_rkv[2]), bf(w1), bf(w2), bf(a1), bf(a2),
            bf(g1), bf(g2), w0.reshape(1, d), a0.reshape(1, d)]
    f32_out = jax.ShapeDtypeStruct(x.shape, F32)
    return pl.pallas_call(
        _rwkv_proj_kernel,
        grid=(b, s // tm),
        in_specs=[xspec] + [const(a) for a in args],
        out_specs=[xspec] * 6,
        out_shape=[f32_out] * 5 + [jax.ShapeDtypeStruct(x.shape, BF16)],
        scratch_shapes=[pltpu.VMEM((tm + SUBLANES, d), F32), pltpu.VMEM((SUBLANES, d), F32)],
        compiler_params=_cparams("arbitrary", "arbitrary"),
        name="rwkv_proj",
    )(x, *args)


def _split(x):
    hi = x.astype(BF16)
    return hi, (x - hi.astype(F32)).astype(BF16)


_BMM_DIMS = {
    "nn": (((2,), (1,)), ((0,), (0,))),
    "nt": (((2,), (2,)), ((0,), (0,))),
}


def _bmm(a, b, passes, form="nn"):
    dims = _BMM_DIMS[form]
    dg = functools.partial(lax.dot_general, dimension_numbers=dims, preferred_element_type=F32)
    if passes == 6:
        return dg(a, b, precision=HIGHEST)
    if passes == 1:
        return dg(a.astype(BF16), b.astype(BF16))
    a_hi, a_lo = _split(a)
    b_hi, b_lo = _split(b)
    return dg(a_hi, b_hi) + (dg(a_hi, b_lo) + dg(a_lo, b_hi))


def _bt(x):
    return jnp.swapaxes(x, 1, 2)


RWKV_PASSES = dict(score=1, inv=3, apply=1, state=1, seq=1)


def _rwkv_rec_kernel(r_ref, k_ref, v_ref, lw_ref, a_ref, kk_ref, ka_ref, rk_ref, lnw_ref, lnb_ref,
                     o_ref, st_ref):
    i = pl.program_id(2)
    tc = r_ref.shape[0]
    ln, dh = D_CHUNK, D_HEAD
    nc = tc // ln
    ps = RWKV_PASSES

    @pl.when(i == 0)
    def _():
        st_ref[...] = jnp.zeros_like(st_ref)

    trow = lax.broadcasted_iota(jnp.int32, (tc, tc), 0)
    tcol = lax.broadcasted_iota(jnp.int32, (tc, tc), 1)
    tri = ((trow // ln == tcol // ln) & (tcol <= trow)).astype(BF16)
    lw = lw_ref[...]
    hi = lw.astype(BF16)
    rem = lw - hi.astype(F32)
    mid = rem.astype(BF16)
    lo = (rem - mid.astype(F32)).astype(BF16)
    g3 = _dot(tri, jnp.concatenate([hi, mid, lo], axis=1))
    gc = g3[:, :LANES] + (g3[:, LANES:2 * LANES] + g3[:, 2 * LANES:])
    g_end = jnp.concatenate(
        [jnp.broadcast_to(gc[(c + 1) * ln - 1:(c + 1) * ln], (ln, LANES)) for c in range(nc)], axis=0)

    lane = lax.broadcasted_iota(jnp.int32, (tc, LANES), 1)
    first = lane < dh

    def head_sum(x):
        s0 = jnp.sum(jnp.where(first, x, 0.0), axis=-1, keepdims=True)
        s1 = jnp.sum(jnp.where(first, 0.0, x), axis=-1, keepdims=True)
        return jnp.where(first, s0, s1)

    r, k, v, a = r_ref[...], k_ref[...], v_ref[...], a_ref[...]
    kk = k * kk_ref[...]
    kk = kk / jnp.maximum(jnp.sqrt(head_sum(kk * kk)), 1e-12)
    k_mod = k * (1.0 + (a - 1.0) * ka_ref[...])
    beta = kk * a
    e_neg = jnp.exp(-gc)
    e_end = jnp.exp(g_end - gc)
    a_t = -kk * jnp.exp(gc - lw)
    r_t = r * jnp.exp(gc)

    first_c = lax.broadcasted_iota(jnp.int32, (ln, LANES), 1) < dh

    def pair(x):
        return jnp.stack([jnp.concatenate([jnp.where(first_c, x[c * ln:(c + 1) * ln], 0.0),
                                           jnp.where(first_c, 0.0, x[c * ln:(c + 1) * ln])], axis=0)
                          for c in range(nc)])

    pl_ = 2 * ln
    a_b, r_b, v_b = pair(a_t), pair(r_t), pair(v)
    bend_b, kend_b = pair(beta * e_end), pair(k_mod * e_end)
    sc = _bmm(jnp.concatenate([a_b, r_b], axis=1),
              jnp.concatenate([pair(beta * e_neg), pair(k_mod * e_neg)], axis=1), ps["score"], "nt")
    row = lax.broadcasted_iota(jnp.int32, (pl_, pl_), 0)
    col = lax.broadcasted_iota(jnp.int32, (pl_, pl_), 1)
    incl, strict = (col % ln) <= (row % ln), (col % ln) < (row % ln)
    eye = (row == col).astype(F32)
    n_ab = jnp.where(strict, sc[:, :pl_, :pl_], 0.0)
    a_ak = jnp.where(strict, sc[:, :pl_, pl_:], 0.0)
    a_rb = jnp.where(incl, sc[:, pl_:, :pl_], 0.0)
    a_rk = jnp.where(incl, sc[:, pl_:, pl_:], 0.0)
    n_d = jnp.where((row // D_SUB) == (col // D_SUB), n_ab, 0.0)
    n_o = n_ab - n_d
    p = eye + n_d
    pw = n_d
    for _ in range(int(math.log2(D_SUB)) - 1):
        pw = _bmm(pw, pw, ps["inv"])
        p = p + _bmm(p, pw, ps["inv"])
    m1 = _bmm(p, n_o, ps["inv"])
    m2 = _bmm(m1, m1, ps["inv"])
    assert D_CHUNK // D_SUB == 4
    av = _bmm(a_ak, v_b, ps["apply"])
    w12 = _bmm(p, jnp.concatenate([a_b, av], axis=2), ps["apply"])
    w12 = w12 + _bmm(m1, w12, ps["apply"])
    w12 = w12 + _bmm(m2, w12, ps["apply"])
    qy = _bmm(a_rb, w12, ps["apply"])
    q_eff = r_b + qy[:, :, :LANES]
    y0 = qy[:, :, LANES:] + _bmm(a_rk, v_b, ps["apply"])
    w1, w2 = w12[:, :, :LANES], w12[:, :, LANES:]
    decay = jnp.stack([jnp.exp(gc[(c + 1) * ln - 1:(c + 1) * ln]) for c in range(nc)])
    trans = eye * decay + _bmm(_bt(w1), bend_b, ps["state"])
    s_add = _bmm(_bt(jnp.concatenate([w2, v_b], axis=1)), jnp.concatenate([bend_b, kend_b], axis=1), ps["state"])

    st = st_ref[...][None]
    ys = []
    for c in range(nc):
        yp = _bmm(q_eff[c:c + 1], st, ps["seq"], "nt")[0] + y0[c]
        ys.append(yp[:ln] + yp[ln:])
        st = _bmm(st, trans[c:c + 1], ps["seq"]) + s_add[c:c + 1]
    st_ref[...] = st[0]
    y = jnp.concatenate(ys, axis=0)

    mu = head_sum(y) * (1.0 / dh)
    yc = y - mu
    var = head_sum(yc * yc) * (1.0 / dh)
    gn = yc * lax.rsqrt(var + D_GN_EPS) * lnw_ref[...] + lnb_ref[...]
    bonus = head_sum(r * k_mod * rk_ref[...]) * v
    o_ref[...] = gn + bonus


def _rwkv_rec(r, k, v, lw, a, k_k, k_a, r_k, lnx_w, lnx_b, tc=512):
    b, s, d = r.shape
    tc = min(tc, s)
    blk = pl.BlockSpec((None, tc, LANES), lambda bi, p, i: (bi, i, p))
    par = pl.BlockSpec((1, LANES), lambda bi, p, i: (0, p))
    params = [t.reshape(1, d) for t in (k_k, k_a, r_k, lnx_w, lnx_b)]
    return pl.pallas_call(
        _rwkv_rec_kernel,
        grid=(b, d // LANES, s // tc),
        in_specs=[blk] * 5 + [par] * 5,
        out_specs=blk,
        out_shape=jax.ShapeDtypeStruct(r.shape, F32),
        scratch_shapes=[pltpu.VMEM((LANES, LANES), F32)],
        compiler_params=_cparams("parallel", "parallel", "arbitrary"),
        name="rwkv_rec",
    )(r, k, v, lw, a, *params)


def _mixer_rwkv7(x, g_pre, g_post, mix, w_rkv, w0, w1, w2, a0, a1, a2, g1, g2, k_k, k_a, r_k, lnx_w, lnx_b, w_o):
    b, s, d = x.shape
    r, k, v, lw, a, g = _rwkv_proj(x, g_pre, mix, w_rkv, w1, w2, a1, a2, g1, g2, w0, a0)
    y = _rwkv_rec(r, k, v, lw, a, k_k, k_a, r_k, lnx_w, lnx_b)
    flat = lambda t: t.reshape(b * s, d)
    return _matmul_norm_res(flat(y), w_o.astype(BF16), g_post, flat(x), gate=flat(g)).reshape(b, s, d)


def kernel(x, mem, positions, ln_gains, mem_norm, mem_w_kv, mem_w_q, mem_w_o, ffn_w_in, ffn_conv_w, ffn_conv_b, ffn_w_out, a_w_qkv, a_w_o, b_w_in, b_w_gate2, b_gate_bias, b_o_norm, b_w_o, c_w_in, c_q_norm, c_w_uq, c_kv_norm, c_w_ukv, c_w_o, d_mix, d_w_rkv, d_w0, d_w1, d_w2, d_a0, d_a1, d_a2, d_g1, d_g2, d_k_k, d_k_a, d_r_k, d_lnx_w, d_lnx_b, d_w_o):
    b, s, d = x.shape
    depth = ln_gains.shape[0]
    mlen = mem.shape[1]
    mwidth = M_HEADS * M_HEAD_DIM
    mkv = _norm_matmul(mem.reshape(b * mlen, d), mem_norm, mem_w_kv.astype(BF16), out_dtype=BF16,
                       tn=2 * mwidth, tm=mlen)
    mkv = mkv.reshape(b, mlen, 2 * mwidth)
    mem_k, mem_v = mkv[:, :, :mwidth], mkv[:, :, mwidth:]
    tabs = _rope_tables(positions)
    tabs_a, tabs_c = tabs[:3], tabs[3:]
    for i in range(depth):
        m, j = i % N_MIXERS, i // N_MIXERS
        gains = ln_gains[i]
        if m == 0:
            x = _mixer_dilated(x, tabs_a, gains[0], gains[1], a_w_qkv[j], a_w_o[j])
        elif m == 1:
            x = _mixer_gla(x, gains[0], gains[1], b_w_in[j], b_w_gate2[j], b_gate_bias[j], b_o_norm[j], b_w_o[j])
        elif m == 2:
            x = _mixer_mla(x, tabs_c, gains[0], gains[1], c_w_in[j], c_q_norm[j], c_w_uq[j], c_kv_norm[j],
                           c_w_ukv[j], c_w_o[j])
        else:
            x = _mixer_rwkv7(x, gains[0], gains[1], d_mix[j], d_w_rkv[j], d_w0[j], d_w1[j], d_w2[j], d_a0[j],
                             d_a1[j], d_a2[j], d_g1[j], d_g2[j], d_k_k[j], d_k_a[j], d_r_k[j], d_lnx_w[j],
                             d_lnx_b[j], d_w_o[j])
        x = _mem_attn(x, gains[2], mem_w_q[i].astype(BF16), mem_k, mem_v, mem_w_o[i].astype(BF16), gains[3])
        x = _conv_ffn(x, gains[4], ffn_w_in[i], ffn_conv_w[i], ffn_conv_b[i], ffn_w_out[i], gains[5])
    return x
```

```python
import functools
import math

import jax
import jax.numpy as jnp
from jax import lax
from jax.experimental import pallas as pl
from jax.experimental.pallas import tpu as pltpu

F32 = jnp.float32
BF16 = jnp.bfloat16
HIGHEST = lax.Precision.HIGHEST

D_MODEL = 1024
N_MIXERS = 4
NORM_EPS = 1e-6
ROPE_THETA = 500000.0
NEG_INF = -1e30

A_HEAD_DIM = 64
A_HEADS = 8
A_GROUPS = ((128, 1), (512, 4), (2048, 16))
A_BLOCK = 128
A_ROT = A_HEAD_DIM // 4
A_WIDTH = A_HEADS * A_HEAD_DIM

B_HEADS = 4
B_KEY_DIM = 512
B_VAL_DIM = 1024
B_DK = B_KEY_DIM // B_HEADS
B_DV = B_VAL_DIM // B_HEADS
B_GATE_RANK = 16
B_GATE_NORMALIZER = 16.0
B_CHUNK = 64

C_HEADS = 16
C_Q_RANK = 384
C_KV_RANK = 256
C_NOPE = 64
C_ROPE = 32
C_VDIM = 64
C_SLOT = 128

D_HEAD = 64
D_HEADS = D_MODEL // D_HEAD
D_GN_EPS = 64e-5
D_CHUNK = 64
D_SUB = 16

M_HEADS = 4
M_HEAD_DIM = 128

D_FF = 2816
CONV_WIDTH = 3
FF_CHUNK = 256

LANES = 128
SUBLANES = 8
VMEM_LIMIT = 56 * 1024 * 1024


def _cparams(*sem):
    return pltpu.CompilerParams(dimension_semantics=sem, vmem_limit_bytes=VMEM_LIMIT)


def _rms(x, eps=NORM_EPS):
    return x * lax.rsqrt(jnp.mean(x * x, axis=-1, keepdims=True) + eps)


def _sigmoid(x):
    return 1.0 / (1.0 + jnp.exp(-x))


def _softplus(x):
    return jnp.maximum(x, 0.0) + jnp.log(1.0 + jnp.exp(-jnp.abs(x)))


def _dot(a, b, **kw):
    return jnp.dot(a, b, preferred_element_type=F32, **kw)


def _dot_t(a, b, **kw):
    return lax.dot_general(a, b, (((1,), (1,)), ((), ())), preferred_element_type=F32, **kw)


def _tdot(a, b, **kw):
    return lax.dot_general(a, b, (((0,), (0,)), ((), ())), preferred_element_type=F32, **kw)


def _chunk_cumsum(x, ln):
    rows, cols = x.shape
    r = lax.broadcasted_iota(jnp.int32, (rows, rows), 0)
    c = lax.broadcasted_iota(jnp.int32, (rows, rows), 1)
    tri = ((r // ln == c // ln) & (c <= r)).astype(BF16)
    hi = x.astype(BF16)
    rem = x - hi.astype(F32)
    mid = rem.astype(BF16)
    lo = (rem - mid.astype(F32)).astype(BF16)
    g3 = _dot(tri, jnp.concatenate([hi, mid, lo], axis=1))
    return g3[:, :cols] + (g3[:, cols:2 * cols] + g3[:, 2 * cols:])


def _rope_tile(y, c, s1, s2, shift):
    return y * c + pltpu.roll(y, LANES - shift, 1) * s1 + pltpu.roll(y, shift, 1) * s2


def _rope_table_kernel(pos_ref, ca_ref, sa1_ref, sa2_ref, cc_ref, sc1_ref, sc2_ref):
    pos = pos_ref[...].astype(F32)
    lane = lax.broadcasted_iota(jnp.int32, (1, LANES), 1)
    log_theta = math.log(ROPE_THETA)

    def tables(rel, half, rot):
        in_span = (rel >= 0) & (rel < rot)
        idx = jnp.where(in_span, jnp.where(rel < half, rel, rel - half), 0).astype(F32)
        inv_freq = jnp.exp(-(idx * (2.0 / rot)) * log_theta)
        ang = pos * inv_freq
        cos, sin = jnp.cos(ang), jnp.sin(ang)
        first = (rel >= 0) & (rel < half)
        second = (rel >= half) & (rel < rot)
        return (jnp.where(in_span, cos, 1.0), jnp.where(first, -sin, 0.0), jnp.where(second, sin, 0.0))

    ca, sa1, sa2 = tables(lane % A_HEAD_DIM, A_ROT // 2, A_ROT)
    cc, sc1, sc2 = tables(lane - C_NOPE, C_ROPE // 2, C_ROPE)
    ca_ref[...], sa1_ref[...], sa2_ref[...] = ca, sa1, sa2
    cc_ref[...], sc1_ref[...], sc2_ref[...] = cc, sc1, sc2


def _rope_tables(positions):
    t = positions.size
    tm = min(t, 1024)
    pos = positions.reshape(t, 1)
    out = jax.ShapeDtypeStruct((t, LANES), F32)
    spec = pl.BlockSpec((tm, LANES), lambda i: (i, 0))
    return pl.pallas_call(
        _rope_table_kernel,
        grid=(t // tm,),
        in_specs=[pl.BlockSpec((tm, 1), lambda i: (i, 0))],
        out_specs=[spec] * 6,
        out_shape=[out] * 6,
        compiler_params=_cparams("parallel"),
        name="rope_tables",
    )(pos)


def _norm_matmul_kernel(x_ref, g_ref, w_ref, o_ref, xn_ref):
    @pl.when(pl.program_id(1) == 0)
    def _():
        xn_ref[...] = (_rms(x_ref[...]) * g_ref[...]).astype(BF16)

    o_ref[...] = _dot(xn_ref[...], w_ref[...]).astype(o_ref.dtype)


def _norm_matmul(x, g, w, *, out_dtype, tn, tm=1024):
    t, k = x.shape
    n = w.shape[1]
    tm = min(tm, t)
    return pl.pallas_call(
        _norm_matmul_kernel,
        grid=(t // tm, n // tn),
        in_specs=[
            pl.BlockSpec((tm, k), lambda i, j: (i, 0)),
            pl.BlockSpec((1, k), lambda i, j: (0, 0)),
            pl.BlockSpec((k, tn), lambda i, j: (0, j)),
        ],
        out_specs=pl.BlockSpec((tm, tn), lambda i, j: (i, j)),
        out_shape=jax.ShapeDtypeStruct((t, n), out_dtype),
        scratch_shapes=[pltpu.VMEM((tm, k), BF16)],
        compiler_params=_cparams("parallel", "arbitrary"),
        name="norm_matmul",
    )(x, g.reshape(1, k), w)


def _matmul_norm_res_kernel(*refs, gated):
    if gated:
        h_ref, gate_ref, w_ref, g_ref, res_ref, o_ref = refs
        h = (h_ref[...] * gate_ref[...].astype(F32)).astype(BF16)
    else:
        h_ref, w_ref, g_ref, res_ref, o_ref = refs
        h = h_ref[...]
    y = _dot(h, w_ref[...])
    o_ref[...] = res_ref[...] + _rms(y) * g_ref[...]


def _matmul_norm_res(h, w, g, res, gate=None, tm=512):
    t, k = h.shape
    n = w.shape[1]
    tm = min(tm, t)
    row = lambda width: pl.BlockSpec((tm, width), lambda i: (i, 0))
    full = lambda a: pl.BlockSpec(a.shape, lambda i: (0, 0))
    g2 = g.reshape(1, n)
    if gate is None:
        args, specs = [h, w, g2, res], [row(k), full(w), full(g2), row(n)]
    else:
        args, specs = [h, gate, w, g2, res], [row(k), row(k), full(w), full(g2), row(n)]
    return pl.pallas_call(
        functools.partial(_matmul_norm_res_kernel, gated=gate is not None),
        grid=(t // tm,),
        in_specs=specs,
        out_specs=row(n),
        out_shape=jax.ShapeDtypeStruct((t, n), F32),
        compiler_params=_cparams("parallel"),
        name="matmul_norm_res",
    )(*args)


def _mem_attn_kernel(x_ref, gpre_ref, wq_ref, k_ref, v_ref, wo_ref, gpost_ref, o_ref):
    x = x_ref[...]
    hn = (_rms(x) * gpre_ref[...]).astype(BF16)
    q = (_dot(hn, wq_ref[...]) * (M_HEAD_DIM ** -0.5)).astype(BF16)
    k, v = k_ref[...], v_ref[...]
    outs = []
    for h in range(M_HEADS):
        sl = slice(h * M_HEAD_DIM, (h + 1) * M_HEAD_DIM)
        s = _dot_t(q[:, sl], k[:, sl])
        p = jnp.exp(s - jnp.max(s, axis=-1, keepdims=True))
        p = p / jnp.sum(p, axis=-1, keepdims=True)
        outs.append(_dot(p.astype(BF16), v[:, sl]))
    o = jnp.concatenate(outs, axis=1).astype(BF16)
    y = _dot(o, wo_ref[...])
    o_ref[...] = x + _rms(y) * gpost_ref[...]


def _mem_attn(x, gpre, wq, mem_k, mem_v, wo, gpost, tm=512):
    b, s, d = x.shape
    tm = min(tm, s)
    width = M_HEADS * M_HEAD_DIM
    mlen = mem_k.shape[1]
    const = lambda a: pl.BlockSpec(a.shape, lambda bi, i: (0,) * a.ndim)
    gpre, gpost = gpre.reshape(1, d), gpost.reshape(1, d)
    xspec = pl.BlockSpec((None, tm, d), lambda bi, i: (bi, i, 0))
    mspec = pl.BlockSpec((None, mlen, width), lambda bi, i: (bi, 0, 0))
    return pl.pallas_call(
        _mem_attn_kernel,
        grid=(b, s // tm),
        in_specs=[xspec, const(gpre), const(wq), mspec, mspec, const(wo), const(gpost)],
        out_specs=xspec,
        out_shape=jax.ShapeDtypeStruct(x.shape, F32),
        compiler_params=_cparams("parallel", "parallel"),
        name="mem_attn",
    )(x, gpre, wq, mem_k, mem_v, wo, gpost)


def _conv_ffn_kernel(x_ref, gpre_ref, win_ref, cw_ref, cb_ref, wout_ref, gpost_ref, o_ref, carry_ref):
    i = pl.program_id(1)
    tm = x_ref.shape[0]
    fc = FF_CHUNK

    @pl.when(i == 0)
    def _():
        carry_ref[...] = jnp.zeros_like(carry_ref)

    x = x_ref[...]
    hn = (_rms(x) * gpre_ref[...]).astype(BF16)
    acc = jnp.zeros((tm, x.shape[1]), F32)
    nc = D_FF // fc
    up = lambda j: _dot(hn, win_ref[:, 2 * j * fc:2 * (j + 1) * fc])
    u_next = up(0)
    for j in range(nc):
        cols = slice(2 * j * fc, 2 * (j + 1) * fc)
        u = u_next
        if j + 1 < nc:
            u_next = up(j + 1)
        ext = jnp.concatenate([carry_ref[:, cols], u], axis=0)
        carry_ref[:, cols] = u[tm - SUBLANES:, :]
        cw = cw_ref[:, cols]
        conv = (cb_ref[:, cols] + ext[SUBLANES - 2:SUBLANES - 2 + tm] * cw[0:1]
                + ext[SUBLANES - 1:SUBLANES - 1 + tm] * cw[1:2] + u * cw[2:3])
        gate, val = conv[:, :fc], conv[:, fc:]
        h = (gate * _sigmoid(gate) * val).astype(BF16)
        acc = acc + _dot(h, wout_ref[j * fc:(j + 1) * fc, :])
    o_ref[...] = x + _rms(acc) * gpost_ref[...]


def _conv_ffn(x, gpre, w_in, conv_w, conv_b, w_out, gpost, tm=512):
    b, s, d = x.shape
    tm = min(tm, s)
    fc = FF_CHUNK
    nc = D_FF // fc

    def interleave(a):
        lead = a.shape[:-1]
        a = a.reshape(lead + (2, nc, fc))
        return jnp.swapaxes(a, -3, -2).reshape(lead + (2 * D_FF,))

    w_in = interleave(w_in).astype(BF16)
    conv_w = interleave(conv_w)
    conv_b = interleave(conv_b).reshape(1, 2 * D_FF)
    w_out = w_out.astype(BF16)
    gpre, gpost = gpre.reshape(1, d), gpost.reshape(1, d)
    xspec = pl.BlockSpec((None, tm, d), lambda bi, i: (bi, i, 0))
    const = lambda a: pl.BlockSpec(a.shape, lambda bi, i: (0, 0), pipeline_mode=pl.Buffered(1))
    return pl.pallas_call(
        _conv_ffn_kernel,
        grid=(b, s // tm),
        in_specs=[xspec, const(gpre), const(w_in), const(conv_w), const(conv_b), const(w_out), const(gpost)],
        out_specs=xspec,
        out_shape=jax.ShapeDtypeStruct(x.shape, F32),
        scratch_shapes=[pltpu.VMEM((SUBLANES, 2 * D_FF), F32)],
        compiler_params=_cparams("arbitrary", "arbitrary"),
        name="conv_ffn",
    )(x, gpre, w_in, conv_w, conv_b, w_out, gpost)


def _dilated_qkv_kernel(x_ref, g_ref, w_ref, c_ref, s1_ref, s2_ref, o_ref, xn_ref, ybuf_ref, *, dil):
    sec = pl.program_id(2)
    tm = x_ref.shape[0]

    @pl.when(sec == 0)
    def _():
        xn_ref[...] = (_rms(x_ref[...]) * g_ref[...]).astype(BF16)

    y = _dot(xn_ref[...], w_ref[...])
    on = sec < 2
    c = jnp.where(on, c_ref[...], 1.0)
    s1 = jnp.where(on, s1_ref[...], 0.0)
    s2 = jnp.where(on, s2_ref[...], 0.0)
    scale = jnp.where(sec == 0, A_HEAD_DIM ** -0.5, 1.0)
    y = jnp.concatenate([_rope_tile(y[:, t * LANES:(t + 1) * LANES], c, s1, s2, A_ROT // 2)
                         for t in range(A_WIDTH // LANES)], axis=1) * scale
    if dil == 1:
        o_ref[0] = y.astype(BF16)
    else:
        nt = A_WIDTH // LANES
        for t in range(nt):
            ybuf_ref[t] = y[:, t * LANES:(t + 1) * LANES]
        for r in range(dil):
            o_ref[r] = jnp.concatenate(
                [ybuf_ref[t, pl.ds(r, tm // dil, stride=dil), :] for t in range(nt)], axis=1).astype(BF16)


def _dilated_qkv(x, g_pre, w, tabs, dil, tm=1024):
    b, s, d = x.shape
    tm = min(tm, s)
    nblk = s // tm
    tab = pl.BlockSpec((tm, LANES), lambda bi, i, sec: (bi * nblk + i, 0))
    return pl.pallas_call(
        functools.partial(_dilated_qkv_kernel, dil=dil),
        grid=(b, nblk, 3),
        in_specs=[
            pl.BlockSpec((None, tm, d), lambda bi, i, sec: (bi, i, 0)),
            pl.BlockSpec((1, d), lambda bi, i, sec: (0, 0)),
            pl.BlockSpec((d, A_WIDTH), lambda bi, i, sec: (0, sec)),
            tab, tab, tab,
        ],
        out_specs=pl.BlockSpec((None, dil, tm // dil, A_WIDTH), lambda bi, i, sec: (bi, 0, i, sec)),
        out_shape=jax.ShapeDtypeStruct((b, dil, s // dil, 3 * A_WIDTH), BF16),
        scratch_shapes=[pltpu.VMEM((tm, d), BF16), pltpu.VMEM((A_WIDTH // LANES, tm, LANES), F32)],
        compiler_params=_cparams("parallel", "parallel", "arbitrary"),
        name="dilated_qkv",
    )(x, g_pre.reshape(1, d), w, *tabs)


A_STEP_BLOCKS = 4


def _dilated_attn_kernel(q_ref, kc_ref, kp_ref, vc_ref, vp_ref, o_ref, lse_ref):
    n = pl.program_id(2)
    qi = lax.broadcasted_iota(jnp.int32, (A_BLOCK, A_BLOCK), 0)
    kj = lax.broadcasted_iota(jnp.int32, (A_BLOCK, A_BLOCK), 1)
    band = kj >= qi
    mask_c = kj <= qi
    first = lax.broadcasted_iota(jnp.int32, (A_BLOCK, LANES), 1) < A_HEAD_DIM
    zero = jnp.zeros((), BF16)
    nblk = q_ref.shape[0] // A_BLOCK
    nslab = A_WIDTH // LANES
    tiles = [(j, hp) for j in range(nblk) for hp in range(nslab)]
    rows = lambda j: slice(j * A_BLOCK, (j + 1) * A_BLOCK)
    lanes = lambda hp: slice(hp * LANES, (hp + 1) * LANES)
    grab = lambda ref: jnp.stack([ref[rows(j), lanes(hp)] for j, hp in tiles])
    q, kc, vc = grab(q_ref), grab(kc_ref), grab(vc_ref)
    prev_of = lambda cur_ref, prev_ref: jnp.stack(
        [prev_ref[:, lanes(hp)] if j == 0 else cur_ref[rows(j - 1), lanes(hp)] for j, hp in tiles])
    kp, vp = prev_of(kc_ref, kp_ref), prev_of(vc_ref, vp_ref)
    nt_dims, nn_dims = _BMM_DIMS["nt"], _BMM_DIMS["nn"]
    bdot = lambda a, b, dims: lax.dot_general(a, b, dims, preferred_element_type=F32)
    o_pair = lse_pair = None
    for h in range(2):
        sel = first if h == 0 else jnp.logical_not(first)
        qh = jnp.where(sel, q, zero)
        s_p = jnp.where(band, bdot(qh, kp, nt_dims), NEG_INF)
        s_p = jnp.concatenate([jnp.where(n > 0, s_p[:nslab], NEG_INF), s_p[nslab:]], axis=0)
        s_c = jnp.where(mask_c, bdot(qh, kc, nt_dims), NEG_INF)
        m = jnp.max(jnp.maximum(s_p, s_c), axis=-1, keepdims=True)
        p_p, p_c = jnp.exp(s_p - m), jnp.exp(s_c - m)
        l = jnp.sum(p_p + p_c, axis=-1, keepdims=True)
        o = (bdot(p_p.astype(BF16), jnp.where(sel, vp, zero), nn_dims)
             + bdot(p_c.astype(BF16), jnp.where(sel, vc, zero), nn_dims)) / l
        lse = jnp.broadcast_to(m + jnp.log(l), o.shape)
        o_pair = o if h == 0 else o_pair + o
        lse_pair = lse if h == 0 else jnp.where(first, lse_pair, lse)
    for t, (j, hp) in enumerate(tiles):
        o_ref[rows(j), lanes(hp)] = o_pair[t].astype(o_ref.dtype)
        lse_ref[rows(j), lanes(hp)] = lse_pair[t]


def _dilated_attn(qkv):
    b, dil, ln, _ = qkv.shape
    nblk = min(A_STEP_BLOCKS, ln // A_BLOCK)
    tq = nblk * A_BLOCK

    def cur(section):
        return pl.BlockSpec((None, None, tq, A_WIDTH), lambda bi, r, n: (bi, r, n, section))

    def prev(section):
        return pl.BlockSpec((None, None, A_BLOCK, A_WIDTH),
                            lambda bi, r, n: (bi, r, jnp.maximum(n * nblk - 1, 0), section))

    return pl.pallas_call(
        _dilated_attn_kernel,
        grid=(b, dil, ln // tq),
        in_specs=[cur(0), cur(1), prev(1), cur(2), prev(2)],
        out_specs=[cur(0), cur(0)],
        out_shape=[jax.ShapeDtypeStruct((b, dil, ln, A_WIDTH), BF16),
                   jax.ShapeDtypeStruct((b, dil, ln, A_WIDTH), F32)],
        compiler_params=_cparams("parallel", "parallel", "arbitrary"),
        name="dilated_attn",
    )(qkv, qkv, qkv, qkv, qkv)


def _dilated_combine_kernel(*refs, dils):
    ng = len(dils)
    o_refs, l_refs = refs[:ng], refs[ng:2 * ng]
    w_ref, g_ref, res_ref, out_ref = refs[2 * ng:2 * ng + 4]
    bufs = list(refs[2 * ng + 4:])
    tm = res_ref.shape[0]

    def natural(ref, dil):
        if dil == 1:
            return ref[0].astype(F32)
        buf = bufs.pop()
        nt = A_WIDTH // LANES
        for r in range(dil):
            rows = ref[r].astype(F32)
            for t in range(nt):
                buf[t, pl.ds(r, tm // dil, stride=dil), :] = rows[:, t * LANES:(t + 1) * LANES]
        return jnp.concatenate([buf[t] for t in range(nt)], axis=1)

    os_ = [natural(r, d) for r, d in zip(o_refs, dils)]
    ls = [natural(r, d) for r, d in zip(l_refs, dils)]
    m = functools.reduce(jnp.maximum, ls)
    es = [jnp.exp(l - m) for l in ls]
    o = sum(e * o for e, o in zip(es, os_)) / sum(es)
    y = _dot(o.astype(BF16), w_ref[...])
    out_ref[...] = res_ref[...] + _rms(y) * g_ref[...]


def _dilated_combine(os_, lses, w_o, g, res, tm=512):
    b, s, d = res.shape
    tm = min(tm, s)
    dils = tuple(o.shape[1] for o in os_)
    cls = lambda dil: pl.BlockSpec((None, dil, tm // dil, A_WIDTH), lambda bi, i: (bi, 0, i, 0))
    xspec = pl.BlockSpec((None, tm, d), lambda bi, i: (bi, i, 0))
    n_buf = 2 * sum(dil > 1 for dil in dils)
    return pl.pallas_call(
        functools.partial(_dilated_combine_kernel, dils=dils),
        grid=(b, s // tm),
        in_specs=[cls(dil) for dil in dils] * 2 + [pl.BlockSpec(w_o.shape, lambda bi, i: (0, 0)),
                                                   pl.BlockSpec((1, d), lambda bi, i: (0, 0)), xspec],
        out_specs=xspec,
        out_shape=jax.ShapeDtypeStruct(res.shape, F32),
        scratch_shapes=[pltpu.VMEM((A_WIDTH // LANES, tm, LANES), F32)] * n_buf,
        compiler_params=_cparams("parallel", "parallel"),
        name="dilated_combine",
    )(*os_, *lses, w_o, g.reshape(1, d), res)


def _mixer_dilated(x, tabs_a, g_pre, g_post, w_qkv, w_o):
    ngrp = len(A_GROUPS)
    w = w_qkv.astype(BF16).reshape(w_qkv.shape[0], 3, ngrp, A_WIDTH)
    os_, lses = [], []
    for g, (_, dil) in enumerate(A_GROUPS):
        qkv = _dilated_qkv(x, g_pre, w[:, :, g].reshape(-1, 3 * A_WIDTH), tabs_a, dil)
        o, lse = _dilated_attn(qkv)
        os_.append(o)
        lses.append(lse)
    return _dilated_combine(os_, lses, w_o.astype(BF16), g_post, x)


def _gla_kernel(q_ref, k_ref, v_ref, go_ref, glr_ref, w2_ref, bg_ref, on_ref, o_ref, st_ref):
    i = pl.program_id(2)
    tc = q_ref.shape[0]
    c_len = B_CHUNK

    @pl.when(i == 0)
    def _():
        st_ref[...] = jnp.zeros_like(st_ref)

    nc = tc // c_len
    z = _dot(glr_ref[...].astype(BF16), w2_ref[...]) + bg_ref[...]
    log_a = -_softplus(-z) * (1.0 / B_GATE_NORMALIZER)
    gcum = _chunk_cumsum(log_a, c_len)
    g_end = jnp.concatenate(
        [jnp.broadcast_to(gcum[(c + 1) * c_len - 1:(c + 1) * c_len], (c_len, B_DK)) for c in range(nc)], axis=0)
    q = q_ref[...].astype(F32)
    k = k_ref[...].astype(F32)
    chunks = lambda x: x.reshape(nc, c_len, x.shape[1])
    q_dec = chunks((q * (B_DK ** -0.5) * jnp.exp(gcum)).astype(BF16))
    k_inv = chunks((k * jnp.exp(-gcum)).astype(BF16))
    k_end = chunks(k * jnp.exp(g_end - gcum))
    v = chunks(v_ref[...])
    row = lax.broadcasted_iota(jnp.int32, (c_len, c_len), 0)
    col = lax.broadcasted_iota(jnp.int32, (c_len, c_len), 1)
    bdot = lambda a, b, form: lax.dot_general(a, b, _BMM_DIMS[form], preferred_element_type=F32)
    att = jnp.where(col <= row, bdot(q_dec, k_inv, "nt"), 0.0).astype(BF16)
    o = bdot(att, v, "nn")
    kv = bdot(_bt(v.astype(F32)).astype(BF16), k_end.astype(BF16), "nn")
    st = st_ref[...]
    states = []
    for c in range(nc):
        states.append(st.astype(BF16))
        st = st * jnp.exp(gcum[(c + 1) * c_len - 1:(c + 1) * c_len]) + kv[c]
    st_ref[...] = st
    o = (o + bdot(q_dec, jnp.stack(states), "nt")).reshape(tc, B_DV)
    go = go_ref[...].astype(F32)
    o_ref[...] = (_rms(o) * on_ref[...] * (go * _sigmoid(go))).astype(o_ref.dtype)


def _gla(proj, glr, w_gate2, b_gate, o_norm, tc=512):
    b, s, _ = proj.shape
    tc = min(tc, s)
    nkb = B_KEY_DIM // B_DK
    nvb = 2 * B_KEY_DIM // B_DV
    ngb = nvb + B_VAL_DIM // B_DV
    return pl.pallas_call(
        _gla_kernel,
        grid=(b, B_HEADS, s // tc),
        in_specs=[
            pl.BlockSpec((None, tc, B_DK), lambda bi, h, i: (bi, i, h)),
            pl.BlockSpec((None, tc, B_DK), lambda bi, h, i: (bi, i, nkb + h)),
            pl.BlockSpec((None, tc, B_DV), lambda bi, h, i: (bi, i, nvb + h)),
            pl.BlockSpec((None, tc, B_DV), lambda bi, h, i: (bi, i, ngb + h)),
            pl.BlockSpec((None, tc, LANES), lambda bi, h, i: (bi, i, 0)),
            pl.BlockSpec((LANES, B_DK), lambda bi, h, i: (0, h)),
            pl.BlockSpec((1, B_DK), lambda bi, h, i: (0, h)),
            pl.BlockSpec((1, B_DV), lambda bi, h, i: (0, 0)),
        ],
        out_specs=pl.BlockSpec((None, tc, B_DV), lambda bi, h, i: (bi, i, h)),
        out_shape=jax.ShapeDtypeStruct((b, s, B_VAL_DIM), BF16),
        scratch_shapes=[pltpu.VMEM((B_DV, B_DK), F32)],
        compiler_params=_cparams("parallel", "parallel", "arbitrary"),
        name="gla",
    )(proj, proj, proj, proj, glr, w_gate2, b_gate, o_norm)


def _mixer_gla(x, g_pre, g_post, w_in, w_gate2, b_gate, o_norm, w_o):
    b, s, d = x.shape
    xf = x.reshape(b * s, d)
    main = 2 * B_KEY_DIM + 2 * B_VAL_DIM
    w_main = w_in[:, :main].astype(BF16)
    w_lr = jnp.pad(w_in[:, main:], ((0, 0), (0, LANES - B_GATE_RANK))).astype(BF16)
    proj = _norm_matmul(xf, g_pre, w_main, out_dtype=BF16, tn=main // 4)
    glr = _norm_matmul(xf, g_pre, w_lr, out_dtype=F32, tn=LANES)
    w2 = jnp.pad(w_gate2, ((0, LANES - B_GATE_RANK), (0, 0))).astype(BF16)
    og = _gla(proj.reshape(b, s, main), glr.reshape(b, s, LANES), w2, b_gate.reshape(1, B_KEY_DIM),
              o_norm.reshape(1, B_DV))
    return _matmul_norm_res(og.reshape(b * s, B_VAL_DIM), w_o.astype(BF16), g_post, xf).reshape(b, s, d)


def _mla_proj_kernel(c_ref, qn_ref, kvn_ref, wq_ref, wk_ref, wv_ref, cc_ref, s1_ref, s2_ref,
                     q_ref, k_ref, v_ref):
    c = c_ref[...]
    ckv = c[:, :C_KV_RANK]
    kpe = c[:, C_KV_RANK:C_KV_RANK + C_SLOT]
    cq = c[:, C_KV_RANK + C_SLOT:]
    cc, s1, s2 = cc_ref[...], s1_ref[...], s2_ref[...]
    shift = C_ROPE // 2
    scale = (C_NOPE + C_ROPE) ** -0.5
    q = _dot((_rms(cq) * qn_ref[...]).astype(BF16), wq_ref[...])
    ckvn = (_rms(ckv) * kvn_ref[...]).astype(BF16)
    kn = _dot(ckvn, wk_ref[...])
    kpe = _rope_tile(kpe, cc, s1, s2, shift)
    for h in range(C_HEADS):
        sl = slice(h * C_SLOT, (h + 1) * C_SLOT)
        q_ref[:, sl] = (_rope_tile(q[:, sl], cc, s1, s2, shift) * scale).astype(BF16)
        k_ref[:, sl] = (kn[:, sl] + kpe).astype(BF16)
    v_ref[...] = _dot(ckvn, wv_ref[...]).astype(BF16)


def _mla_proj(c, q_norm, kv_norm, wq, wk, wv, tabs, tm=512):
    t = c.shape[0]
    tm = min(tm, t)
    row = lambda width: pl.BlockSpec((tm, width), lambda i: (i, 0))
    full = lambda a: pl.BlockSpec(a.shape, lambda i: (0, 0))
    qn, kvn = q_norm.reshape(1, C_Q_RANK), kv_norm.reshape(1, C_KV_RANK)
    hw = C_HEADS * C_SLOT
    return pl.pallas_call(
        _mla_proj_kernel,
        grid=(t // tm,),
        in_specs=[row(c.shape[1]), full(qn), full(kvn), full(wq), full(wk), full(wv)] + [row(LANES)] * 3,
        out_specs=[row(hw), row(hw), row(C_HEADS * C_VDIM)],
        out_shape=[jax.ShapeDtypeStruct((t, hw), BF16), jax.ShapeDtypeStruct((t, hw), BF16),
                   jax.ShapeDtypeStruct((t, C_HEADS * C_VDIM), BF16)],
        compiler_params=_cparams("parallel"),
        name="mla_proj",
    )(c, qn, kvn, wq, wk, wv, *tabs)


def _mla_flash_kernel(q_ref, k_ref, v_ref, o_ref, m_ref, l_ref, acc_ref, *, tk):
    i = pl.program_id(2)
    tq = q_ref.shape[0]
    assert tq == tk
    nt = tk // LANES
    m_ref[...] = jnp.full_like(m_ref, NEG_INF)
    l_ref[...] = jnp.zeros_like(l_ref)
    acc_ref[...] = jnp.zeros_like(acc_ref)

    def lane_tiles(x):
        return [x[:, t * LANES:(t + 1) * LANES] for t in range(nt)]

    def step(j, masked):
        start = pl.multiple_of(j * tk, tk)
        v = v_ref[pl.ds(start, tk), :]
        for h in range(2):
            s = _dot_t(q_ref[:, h * C_SLOT:(h + 1) * C_SLOT], k_ref[pl.ds(start, tk), h * C_SLOT:(h + 1) * C_SLOT])
            if masked:
                qi = lax.broadcasted_iota(jnp.int32, (tq, tk), 0)
                kj = lax.broadcasted_iota(jnp.int32, (tq, tk), 1)
                s = jnp.where(kj <= qi, s, NEG_INF)
            tile_max = functools.reduce(jnp.maximum, lane_tiles(s))
            m_old = m_ref[h]
            m_new = jnp.maximum(m_old, jnp.broadcast_to(jnp.max(tile_max, axis=-1, keepdims=True), (tq, LANES)))
            alpha = jnp.exp(m_old - m_new)
            p = jnp.exp(s - jnp.tile(m_new, (1, nt)))
            l_ref[h] = alpha * l_ref[h] + functools.reduce(jnp.add, lane_tiles(p))
            acc_ref[h] = alpha * acc_ref[h] + _dot(p.astype(BF16), v)
            m_ref[h] = m_new

    def body(jj, carry):
        step(2 * jj, False)
        step(2 * jj + 1, False)
        return carry

    lax.fori_loop(0, i // 2, body, 0)

    @pl.when(i % 2 == 1)
    def _():
        step(i - 1, False)

    step(i, True)
    lane = lax.broadcasted_iota(jnp.int32, (tq, 2 * C_VDIM), 1)
    res = [acc_ref[h] / jnp.sum(l_ref[h], axis=-1, keepdims=True) for h in range(2)]
    o_ref[...] = jnp.where(lane < C_VDIM, res[0], res[1]).astype(o_ref.dtype)


def _mla_flash(q, k, v, tq=1024):
    b, s, _ = q.shape
    tq = min(tq, s)
    pairs = C_HEADS // 2
    return pl.pallas_call(
        functools.partial(_mla_flash_kernel, tk=tq),
        grid=(b, pairs, s // tq),
        in_specs=[
            pl.BlockSpec((None, tq, 2 * C_SLOT), lambda bi, p, i: (bi, i, p)),
            pl.BlockSpec((None, s, 2 * C_SLOT), lambda bi, p, i: (bi, 0, p)),
            pl.BlockSpec((None, s, 2 * C_VDIM), lambda bi, p, i: (bi, 0, p)),
        ],
        out_specs=pl.BlockSpec((None, tq, 2 * C_VDIM), lambda bi, p, i: (bi, i, p)),
        out_shape=jax.ShapeDtypeStruct((b, s, C_HEADS * C_VDIM), BF16),
        scratch_shapes=[pltpu.VMEM((2, tq, LANES), F32)] * 3,
        compiler_params=_cparams("parallel", "parallel", "arbitrary"),
        name="mla_flash",
    )(q, k, v)


def _mixer_mla(x, tabs_c, g_pre, g_post, w_in, q_norm, w_uq, kv_norm, w_ukv, w_o):
    b, s, d = x.shape
    xf = x.reshape(b * s, d)
    kpe_w = jnp.pad(w_in[:, C_Q_RANK + C_KV_RANK:], ((0, 0), (C_NOPE, C_SLOT - C_NOPE - C_ROPE)))
    w_c = jnp.concatenate([w_in[:, C_Q_RANK:C_Q_RANK + C_KV_RANK], kpe_w, w_in[:, :C_Q_RANK]], axis=1)
    c = _norm_matmul(xf, g_pre, w_c.astype(BF16), out_dtype=F32, tn=w_c.shape[1])
    pad = C_SLOT - C_NOPE - C_ROPE
    wq = jnp.pad(w_uq.reshape(C_Q_RANK, C_HEADS, C_NOPE + C_ROPE), ((0, 0), (0, 0), (0, pad)))
    wkv = w_ukv.reshape(C_KV_RANK, C_HEADS, C_NOPE + C_VDIM)
    wk = jnp.pad(wkv[:, :, :C_NOPE], ((0, 0), (0, 0), (0, C_SLOT - C_NOPE)))
    wv = wkv[:, :, C_NOPE:]
    q, k, v = _mla_proj(c, q_norm, kv_norm, wq.reshape(C_Q_RANK, -1).astype(BF16),
                        wk.reshape(C_KV_RANK, -1).astype(BF16), wv.reshape(C_KV_RANK, -1).astype(BF16), tabs_c)
    hw = C_HEADS * C_SLOT
    o = _mla_flash(q.reshape(b, s, hw), k.reshape(b, s, hw), v.reshape(b, s, C_HEADS * C_VDIM))
    return _matmul_norm_res(o.reshape(b * s, C_HEADS * C_VDIM), w_o.astype(BF16), g_post, xf).reshape(b, s, d)


def _rwkv_proj_kernel(x_ref, gpre_ref, mix_ref, wr_ref, wk_ref, wv_ref, w1_ref, w2_ref, a1_ref, a2_ref,
                      g1_ref, g2_ref, w0_ref, a0_ref,
                      r_ref, k_ref, v_ref, lw_ref, a_ref, g_ref, hbuf_ref, carry_ref):
    i = pl.program_id(1)
    tm = x_ref.shape[0]
    hn = _rms(x_ref[...]) * gpre_ref[...]
    @pl.when(i == 0)
    def _():
        hbuf_ref[0:SUBLANES, :] = jnp.zeros((SUBLANES, hn.shape[1]), F32)

    @pl.when(i > 0)
    def _():
        hbuf_ref[0:SUBLANES, :] = carry_ref[...]

    hbuf_ref[SUBLANES:SUBLANES + tm, :] = hn
    carry_ref[...] = hn[tm - SUBLANES:, :]
    xx = hbuf_ref[SUBLANES - 1:SUBLANES - 1 + tm, :] - hn
    mix = mix_ref[...]
    mixed = lambda n: (hn + xx * mix[n:n + 1]).astype(BF16)
    r_ref[...] = _dot(mixed(0), wr_ref[...])
    k_ref[...] = _dot(mixed(2), wk_ref[...])
    v_ref[...] = _dot(mixed(3), wv_ref[...])
    wl = _dot(jnp.tanh(_dot(mixed(1), w1_ref[...])).astype(BF16), w2_ref[...])
    w = -_softplus(-(w0_ref[...] + wl)) - 0.5
    lw_ref[...] = -jnp.exp(w)
    al = _dot(_dot(mixed(4), a1_ref[...]).astype(BF16), a2_ref[...])
    a_ref[...] = _sigmoid(a0_ref[...] + al)
    g_ref[...] = _dot(_sigmoid(_dot(mixed(5), g1_ref[...])).astype(BF16), g2_ref[...]).astype(g_ref.dtype)


def _rwkv_proj(x, gpre, mix, w_rkv, w1, w2, a1, a2, g1, g2, w0, a0, tm=256):
    b, s, d = x.shape
    tm = min(tm, s)
    const = lambda a: pl.BlockSpec(a.shape, lambda bi, i: (0,) * a.ndim)
    xspec = pl.BlockSpec((None, tm, d), lambda bi, i: (bi, i, 0))
    bf = lambda a: a.astype(BF16)
    args = [gpre.reshape(1, d), mix, bf(w_rkv[0]), bf(w_rkv[1]), bf(w_rkv[2]), bf(w1), bf(w2), bf(a1), bf(a2),
            bf(g1), bf(g2), w0.reshape(1, d), a0.reshape(1, d)]
    f32_out = jax.ShapeDtypeStruct(x.shape, F32)
    return pl.pallas_call(
        _rwkv_proj_kernel,
        grid=(b, s // tm),
        in_specs=[xspec] + [const(a) for a in args],
        out_specs=[xspec] * 6,
        out_shape=[f32_out] * 5 + [jax.ShapeDtypeStruct(x.shape, BF16)],
        scratch_shapes=[pltpu.VMEM((tm + SUBLANES, d), F32), pltpu.VMEM((SUBLANES, d), F32)],
        compiler_params=_cparams("arbitrary", "arbitrary"),
        name="rwkv_proj",
    )(x, *args)


def _split(x):
    hi = x.astype(BF16)
    return hi, (x - hi.astype(F32)).astype(BF16)


_BMM_DIMS = {
    "nn": (((2,), (1,)), ((0,), (0,))),
    "nt": (((2,), (2,)), ((0,), (0,))),
}


def _bmm(a, b, passes, form="nn"):
    dims = _BMM_DIMS[form]
    dg = functools.partial(lax.dot_general, dimension_numbers=dims, preferred_element_type=F32)
    if passes == 6:
        return dg(a, b, precision=HIGHEST)
    if passes == 1:
        return dg(a.astype(BF16), b.astype(BF16))
    a_hi, a_lo = _split(a)
    b_hi, b_lo = _split(b)
    return dg(a_hi, b_hi) + (dg(a_hi, b_lo) + dg(a_lo, b_hi))


def _bt(x):
    return jnp.swapaxes(x, 1, 2)


RWKV_PASSES = dict(score=1, inv_low=3, inv_high=1, apply=1, state=1, seq=1)


def _rwkv_rec_kernel(r_ref, k_ref, v_ref, lw_ref, a_ref, kk_ref, ka_ref, rk_ref, lnw_ref, lnb_ref,
                     o_ref, st_ref):
    i = pl.program_id(2)
    tc = r_ref.shape[0]
    ln, dh = D_CHUNK, D_HEAD
    nc = tc // ln
    ps = RWKV_PASSES

    @pl.when(i == 0)
    def _():
        st_ref[...] = jnp.zeros_like(st_ref)

    lw = lw_ref[...]
    gc = _chunk_cumsum(lw, ln)
    g_end = jnp.concatenate(
        [jnp.broadcast_to(gc[(c + 1) * ln - 1:(c + 1) * ln], (ln, LANES)) for c in range(nc)], axis=0)

    lane = lax.broadcasted_iota(jnp.int32, (tc, LANES), 1)
    first = lane < dh

    def head_sum(x):
        s0 = jnp.sum(jnp.where(first, x, 0.0), axis=-1, keepdims=True)
        s1 = jnp.sum(jnp.where(first, 0.0, x), axis=-1, keepdims=True)
        return jnp.where(first, s0, s1)

    r, k, v, a = r_ref[...], k_ref[...], v_ref[...], a_ref[...]
    kk = k * kk_ref[...]
    kk = kk / jnp.maximum(jnp.sqrt(head_sum(kk * kk)), 1e-12)
    k_mod = k * (1.0 + (a - 1.0) * ka_ref[...])
    beta = kk * a
    e_neg = jnp.exp(-gc)
    e_end = jnp.exp(g_end - gc)
    a_t = -kk * jnp.exp(gc - lw)
    r_t = r * jnp.exp(gc)

    first_c = lax.broadcasted_iota(jnp.int32, (ln, LANES), 1) < dh

    def pair(x):
        return jnp.stack([jnp.concatenate([jnp.where(first_c, x[c * ln:(c + 1) * ln], 0.0),
                                           jnp.where(first_c, 0.0, x[c * ln:(c + 1) * ln])], axis=0)
                          for c in range(nc)])

    pl_ = 2 * ln
    a_b, r_b, v_b = pair(a_t), pair(r_t), pair(v)
    bend_b, kend_b = pair(beta * e_end), pair(k_mod * e_end)
    sc = _bmm(jnp.concatenate([a_b, r_b], axis=1),
              jnp.concatenate([pair(beta * e_neg), pair(k_mod * e_neg)], axis=1), ps["score"], "nt")
    row = lax.broadcasted_iota(jnp.int32, (pl_, pl_), 0)
    col = lax.broadcasted_iota(jnp.int32, (pl_, pl_), 1)
    incl, strict = (col % ln) <= (row % ln), (col % ln) < (row % ln)
    eye = (row == col).astype(F32)
    n_ab = jnp.where(strict, sc[:, :pl_, :pl_], 0.0)
    a_ak = jnp.where(strict, sc[:, :pl_, pl_:], 0.0)
    a_rb = jnp.where(incl, sc[:, pl_:, :pl_], 0.0)
    a_rk = jnp.where(incl, sc[:, pl_:, pl_:], 0.0)
    n_d = jnp.where((row // D_SUB) == (col // D_SUB), n_ab, 0.0)
    n_o = n_ab - n_d
    p = eye + n_d
    pw = n_d
    n_sq = int(math.log2(D_SUB)) - 1
    for t in range(n_sq):
        passes = ps["inv_low"] if t < n_sq - 1 else ps["inv_high"]
        pw = _bmm(pw, pw, passes)
        p = p + _bmm(p, pw, passes)
    m1 = _bmm(p, n_o, ps["inv_high"])
    m2 = _bmm(m1, m1, ps["inv_high"])
    assert D_CHUNK // D_SUB == 4
    av = _bmm(a_ak, v_b, ps["apply"])
    w12 = _bmm(p, jnp.concatenate([a_b, av], axis=2), ps["apply"])
    w12 = w12 + _bmm(m1, w12, ps["apply"])
    w12 = w12 + _bmm(m2, w12, ps["apply"])
    qy = _bmm(a_rb, w12, ps["apply"])
    q_eff = r_b + qy[:, :, :LANES]
    y0 = qy[:, :, LANES:] + _bmm(a_rk, v_b, ps["apply"])
    w1, w2 = w12[:, :, :LANES], w12[:, :, LANES:]
    decay = jnp.stack([jnp.exp(gc[(c + 1) * ln - 1:(c + 1) * ln]) for c in range(nc)])
    trans = eye * decay + _bmm(_bt(w1), bend_b, ps["state"])
    s_add = _bmm(_bt(jnp.concatenate([w2, v_b], axis=1)), jnp.concatenate([bend_b, kend_b], axis=1), ps["state"])

    st = st_ref[...][None]
    ys = []
    for c in range(nc):
        yp = _bmm(q_eff[c:c + 1], st, ps["seq"], "nt")[0] + y0[c]
        ys.append(yp[:ln] + yp[ln:])
        st = _bmm(st, trans[c:c + 1], ps["seq"]) + s_add[c:c + 1]
    st_ref[...] = st[0]
    y = jnp.concatenate(ys, axis=0)

    mu = head_sum(y) * (1.0 / dh)
    yc = y - mu
    var = head_sum(yc * yc) * (1.0 / dh)
    gn = yc * lax.rsqrt(var + D_GN_EPS) * lnw_ref[...] + lnb_ref[...]
    bonus = head_sum(r * k_mod * rk_ref[...]) * v
    o_ref[...] = gn + bonus


def _rwkv_rec(r, k, v, lw, a, k_k, k_a, r_k, lnx_w, lnx_b, tc=512):
    b, s, d = r.shape
    tc = min(tc, s)
    blk = pl.BlockSpec((None, tc, LANES), lambda bi, p, i: (bi, i, p))
    par = pl.BlockSpec((1, LANES), lambda bi, p, i: (0, p))
    params = [t.reshape(1, d) for t in (k_k, k_a, r_k, lnx_w, lnx_b)]
    return pl.pallas_call(
        _rwkv_rec_kernel,
        grid=(b, d // LANES, s // tc),
        in_specs=[blk] * 5 + [par] * 5,
        out_specs=blk,
        out_shape=jax.ShapeDtypeStruct(r.shape, F32),
        scratch_shapes=[pltpu.VMEM((LANES, LANES), F32)],
        compiler_params=_cparams("parallel", "parallel", "arbitrary"),
        name="rwkv_rec",
    )(r, k, v, lw, a, *params)


def _mixer_rwkv7(x, g_pre, g_post, mix, w_rkv, w0, w1, w2, a0, a1, a2, g1, g2, k_k, k_a, r_k, lnx_w, lnx_b, w_o):
    b, s, d = x.shape
    r, k, v, lw, a, g = _rwkv_proj(x, g_pre, mix, w_rkv, w1, w2, a1, a2, g1, g2, w0, a0)
    y = _rwkv_rec(r, k, v, lw, a, k_k, k_a, r_k, lnx_w, lnx_b)
    flat = lambda t: t.reshape(b * s, d)
    return _matmul_norm_res(flat(y), w_o.astype(BF16), g_post, flat(x), gate=flat(g)).reshape(b, s, d)


def kernel(x, mem, positions, ln_gains, mem_norm, mem_w_kv, mem_w_q, mem_w_o, ffn_w_in, ffn_conv_w, ffn_conv_b, ffn_w_out, a_w_qkv, a_w_o, b_w_in, b_w_gate2, b_gate_bias, b_o_norm, b_w_o, c_w_in, c_q_norm, c_w_uq, c_kv_norm, c_w_ukv, c_w_o, d_mix, d_w_rkv, d_w0, d_w1, d_w2, d_a0, d_a1, d_a2, d_g1, d_g2, d_k_k, d_k_a, d_r_k, d_lnx_w, d_lnx_b, d_w_o):
    b, s, d = x.shape
    depth = ln_gains.shape[0]
    mlen = mem.shape[1]
    mwidth = M_HEADS * M_HEAD_DIM
    mkv = _norm_matmul(mem.reshape(b * mlen, d), mem_norm, mem_w_kv.astype(BF16), out_dtype=BF16,
                       tn=2 * mwidth, tm=mlen)
    mkv = mkv.reshape(b, mlen, 2 * mwidth)
    mem_k, mem_v = mkv[:, :, :mwidth], mkv[:, :, mwidth:]
    tabs = _rope_tables(positions)
    tabs_a, tabs_c = tabs[:3], tabs[3:]
    for i in range(depth):
        m, j = i % N_MIXERS, i // N_MIXERS
        gains = ln_gains[i]
        if m == 0:
            x = _mixer_dilated(x, tabs_a, gains[0], gains[1], a_w_qkv[j], a_w_o[j])
        elif m == 1:
            x = _mixer_gla(x, gains[0], gains[1], b_w_in[j], b_w_gate2[j], b_gate_bias[j], b_o_norm[j], b_w_o[j])
        elif m == 2:
            x = _mixer_mla(x, tabs_c, gains[0], gains[1], c_w_in[j], c_q_norm[j], c_w_uq[j], c_kv_norm[j],
                           c_w_ukv[j], c_w_o[j])
        else:
            x = _mixer_rwkv7(x, gains[0], gains[1], d_mix[j], d_w_rkv[j], d_w0[j], d_w1[j], d_w2[j], d_a0[j],
                             d_a1[j], d_a2[j], d_g1[j], d_g2[j], d_k_k[j], d_k_a[j], d_r_k[j], d_lnx_w[j],
                             d_lnx_b[j], d_w_o[j])
        x = _mem_attn(x, gains[2], mem_w_q[i].astype(BF16), mem_k, mem_v, mem_w_o[i].astype(BF16), gains[3])
        x = _conv_ffn(x, gains[4], ffn_w_in[i], ffn_conv_w[i], ffn_conv_b[i], ffn_w_out[i], gains[5])
    return x
```

```python
import functools
import math

import jax
import jax.numpy as jnp
from jax import lax
from jax.experimental import pallas as pl
from jax.experimental.pallas import tpu as pltpu

F32 = jnp.float32
BF16 = jnp.bfloat16
HIGHEST = lax.Precision.HIGHEST

D_MODEL = 1024
N_MIXERS = 4
NORM_EPS = 1e-6
ROPE_THETA = 500000.0
NEG_INF = -1e30

A_HEAD_DIM = 64
A_HEADS = 8
A_GROUPS = ((128, 1), (512, 4), (2048, 16))
A_BLOCK = 128
A_ROT = A_HEAD_DIM // 4
A_WIDTH = A_HEADS * A_HEAD_DIM

B_HEADS = 4
B_KEY_DIM = 512
B_VAL_DIM = 1024
B_DK = B_KEY_DIM // B_HEADS
B_DV = B_VAL_DIM // B_HEADS
B_GATE_RANK = 16
B_GATE_NORMALIZER = 16.0
B_CHUNK = 64

C_HEADS = 16
C_Q_RANK = 384
C_KV_RANK = 256
C_NOPE = 64
C_ROPE = 32
C_VDIM = 64
C_SLOT = 128

D_HEAD = 64
D_HEADS = D_MODEL // D_HEAD
D_GN_EPS = 64e-5
D_CHUNK = 64
D_SUB = 16

M_HEADS = 4
M_HEAD_DIM = 128

D_FF = 2816
CONV_WIDTH = 3
FF_CHUNK = 256

LANES = 128
SUBLANES = 8
VMEM_LIMIT = 56 * 1024 * 1024


def _cparams(*sem):
    return pltpu.CompilerParams(dimension_semantics=sem, vmem_limit_bytes=VMEM_LIMIT)


def _rms(x, eps=NORM_EPS):
    return x * lax.rsqrt(jnp.mean(x * x, axis=-1, keepdims=True) + eps)


def _sigmoid(x):
    return 1.0 / (1.0 + jnp.exp(-x))


def _softplus(x):
    return jnp.maximum(x, 0.0) + jnp.log(1.0 + jnp.exp(-jnp.abs(x)))


def _dot(a, b, **kw):
    return jnp.dot(a, b, preferred_element_type=F32, **kw)


def _dot_t(a, b, **kw):
    return lax.dot_general(a, b, (((1,), (1,)), ((), ())), preferred_element_type=F32, **kw)


def _tdot(a, b, **kw):
    return lax.dot_general(a, b, (((0,), (0,)), ((), ())), preferred_element_type=F32, **kw)


def _chunk_cumsum(x, ln):
    rows, cols = x.shape
    r = lax.broadcasted_iota(jnp.int32, (rows, rows), 0)
    c = lax.broadcasted_iota(jnp.int32, (rows, rows), 1)
    tri = ((r // ln == c // ln) & (c <= r)).astype(BF16)
    hi = x.astype(BF16)
    rem = x - hi.astype(F32)
    mid = rem.astype(BF16)
    lo = (rem - mid.astype(F32)).astype(BF16)
    g3 = _dot(tri, jnp.concatenate([hi, mid, lo], axis=1))
    return g3[:, :cols] + (g3[:, cols:2 * cols] + g3[:, 2 * cols:])


def _rope_tile(y, c, s1, s2, shift):
    return y * c + pltpu.roll(y, LANES - shift, 1) * s1 + pltpu.roll(y, shift, 1) * s2


def _rope_table_kernel(pos_ref, ca_ref, sa1_ref, sa2_ref, cc_ref, sc1_ref, sc2_ref):
    pos = pos_ref[...].astype(F32)
    lane = lax.broadcasted_iota(jnp.int32, (1, LANES), 1)
    log_theta = math.log(ROPE_THETA)

    def tables(rel, half, rot):
        in_span = (rel >= 0) & (rel < rot)
        idx = jnp.where(in_span, jnp.where(rel < half, rel, rel - half), 0).astype(F32)
        inv_freq = jnp.exp(-(idx * (2.0 / rot)) * log_theta)
        ang = pos * inv_freq
        cos, sin = jnp.cos(ang), jnp.sin(ang)
        first = (rel >= 0) & (rel < half)
        second = (rel >= half) & (rel < rot)
        return (jnp.where(in_span, cos, 1.0), jnp.where(first, -sin, 0.0), jnp.where(second, sin, 0.0))

    ca, sa1, sa2 = tables(lane % A_HEAD_DIM, A_ROT // 2, A_ROT)
    cc, sc1, sc2 = tables(lane - C_NOPE, C_ROPE // 2, C_ROPE)
    ca_ref[...], sa1_ref[...], sa2_ref[...] = ca, sa1, sa2
    cc_ref[...], sc1_ref[...], sc2_ref[...] = cc, sc1, sc2


def _rope_tables(positions):
    t = positions.size
    tm = min(t, 1024)
    pos = positions.reshape(t, 1)
    out = jax.ShapeDtypeStruct((t, LANES), F32)
    spec = pl.BlockSpec((tm, LANES), lambda i: (i, 0))
    return pl.pallas_call(
        _rope_table_kernel,
        grid=(t // tm,),
        in_specs=[pl.BlockSpec((tm, 1), lambda i: (i, 0))],
        out_specs=[spec] * 6,
        out_shape=[out] * 6,
        compiler_params=_cparams("parallel"),
        name="rope_tables",
    )(pos)


def _norm_matmul_kernel(x_ref, g_ref, w_ref, o_ref, xn_ref):
    @pl.when(pl.program_id(1) == 0)
    def _():
        xn_ref[...] = (_rms(x_ref[...]) * g_ref[...]).astype(BF16)

    o_ref[...] = _dot(xn_ref[...], w_ref[...]).astype(o_ref.dtype)


def _norm_matmul(x, g, w, *, out_dtype, tn, tm=1024):
    t, k = x.shape
    n = w.shape[1]
    tm = min(tm, t)
    return pl.pallas_call(
        _norm_matmul_kernel,
        grid=(t // tm, n // tn),
        in_specs=[
            pl.BlockSpec((tm, k), lambda i, j: (i, 0)),
            pl.BlockSpec((1, k), lambda i, j: (0, 0)),
            pl.BlockSpec((k, tn), lambda i, j: (0, j)),
        ],
        out_specs=pl.BlockSpec((tm, tn), lambda i, j: (i, j)),
        out_shape=jax.ShapeDtypeStruct((t, n), out_dtype),
        scratch_shapes=[pltpu.VMEM((tm, k), BF16)],
        compiler_params=_cparams("parallel", "arbitrary"),
        name="norm_matmul",
    )(x, g.reshape(1, k), w)


def _matmul_norm_res_kernel(*refs, gated):
    if gated:
        h_ref, gate_ref, w_ref, g_ref, res_ref, o_ref = refs
        h = (h_ref[...] * gate_ref[...].astype(F32)).astype(BF16)
    else:
        h_ref, w_ref, g_ref, res_ref, o_ref = refs
        h = h_ref[...]
    y = _dot(h, w_ref[...])
    o_ref[...] = res_ref[...] + _rms(y) * g_ref[...]


def _matmul_norm_res(h, w, g, res, gate=None, tm=512):
    t, k = h.shape
    n = w.shape[1]
    tm = min(tm, t)
    row = lambda width: pl.BlockSpec((tm, width), lambda i: (i, 0))
    full = lambda a: pl.BlockSpec(a.shape, lambda i: (0, 0))
    g2 = g.reshape(1, n)
    if gate is None:
        args, specs = [h, w, g2, res], [row(k), full(w), full(g2), row(n)]
    else:
        args, specs = [h, gate, w, g2, res], [row(k), row(k), full(w), full(g2), row(n)]
    return pl.pallas_call(
        functools.partial(_matmul_norm_res_kernel, gated=gate is not None),
        grid=(t // tm,),
        in_specs=specs,
        out_specs=row(n),
        out_shape=jax.ShapeDtypeStruct((t, n), F32),
        compiler_params=_cparams("parallel"),
        name="matmul_norm_res",
    )(*args)


def _mem_attn_kernel(x_ref, gpre_ref, wq_ref, k_ref, v_ref, wo_ref, gpost_ref, o_ref):
    x = x_ref[...]
    hn = (_rms(x) * gpre_ref[...]).astype(BF16)
    q = (_dot(hn, wq_ref[...]) * (M_HEAD_DIM ** -0.5)).astype(BF16)
    k, v = k_ref[...], v_ref[...]
    heads = lambda t: jnp.stack([t[:, h * M_HEAD_DIM:(h + 1) * M_HEAD_DIM] for h in range(M_HEADS)])
    s = lax.dot_general(heads(q), heads(k), _BMM_DIMS["nt"], preferred_element_type=F32)
    p = jnp.exp(s - jnp.max(s, axis=-1, keepdims=True))
    p = p / jnp.sum(p, axis=-1, keepdims=True)
    o = lax.dot_general(p.astype(BF16), heads(v), _BMM_DIMS["nn"], preferred_element_type=F32)
    o = jnp.concatenate([o[h] for h in range(M_HEADS)], axis=1).astype(BF16)
    y = _dot(o, wo_ref[...])
    o_ref[...] = x + _rms(y) * gpost_ref[...]


def _mem_attn(x, gpre, wq, mem_k, mem_v, wo, gpost, tm=512):
    b, s, d = x.shape
    tm = min(tm, s)
    width = M_HEADS * M_HEAD_DIM
    mlen = mem_k.shape[1]
    const = lambda a: pl.BlockSpec(a.shape, lambda bi, i: (0,) * a.ndim)
    gpre, gpost = gpre.reshape(1, d), gpost.reshape(1, d)
    xspec = pl.BlockSpec((None, tm, d), lambda bi, i: (bi, i, 0))
    mspec = pl.BlockSpec((None, mlen, width), lambda bi, i: (bi, 0, 0))
    return pl.pallas_call(
        _mem_attn_kernel,
        grid=(b, s // tm),
        in_specs=[xspec, const(gpre), const(wq), mspec, mspec, const(wo), const(gpost)],
        out_specs=xspec,
        out_shape=jax.ShapeDtypeStruct(x.shape, F32),
        compiler_params=_cparams("parallel", "parallel"),
        name="mem_attn",
    )(x, gpre, wq, mem_k, mem_v, wo, gpost)


def _conv_ffn_kernel(x_ref, gpre_ref, win_ref, cw_ref, cb_ref, wout_ref, gpost_ref, o_ref, carry_ref):
    i = pl.program_id(1)
    tm = x_ref.shape[0]
    fc = FF_CHUNK

    @pl.when(i == 0)
    def _():
        carry_ref[...] = jnp.zeros_like(carry_ref)

    x = x_ref[...]
    hn = (_rms(x) * gpre_ref[...]).astype(BF16)
    acc = jnp.zeros((tm, x.shape[1]), F32)
    nc = D_FF // fc
    us = [_dot(hn, win_ref[:, 2 * j * fc:2 * (j + 1) * fc]) for j in range(nc)]
    for j in range(nc):
        cols = slice(2 * j * fc, 2 * (j + 1) * fc)
        u = us[j]
        ext = jnp.concatenate([carry_ref[:, cols], u], axis=0)
        carry_ref[:, cols] = u[tm - SUBLANES:, :]
        cw = cw_ref[:, cols]
        conv = (cb_ref[:, cols] + ext[SUBLANES - 2:SUBLANES - 2 + tm] * cw[0:1]
                + ext[SUBLANES - 1:SUBLANES - 1 + tm] * cw[1:2] + u * cw[2:3])
        gate, val = conv[:, :fc], conv[:, fc:]
        h = (gate * _sigmoid(gate) * val).astype(BF16)
        acc = acc + _dot(h, wout_ref[j * fc:(j + 1) * fc, :])
    o_ref[...] = x + _rms(acc) * gpost_ref[...]


def _conv_ffn(x, gpre, w_in, conv_w, conv_b, w_out, gpost, tm=512):
    b, s, d = x.shape
    tm = min(tm, s)
    fc = FF_CHUNK
    nc = D_FF // fc

    def interleave(a):
        lead = a.shape[:-1]
        a = a.reshape(lead + (2, nc, fc))
        return jnp.swapaxes(a, -3, -2).reshape(lead + (2 * D_FF,))

    w_in = interleave(w_in).astype(BF16)
    conv_w = interleave(conv_w)
    conv_b = interleave(conv_b).reshape(1, 2 * D_FF)
    w_out = w_out.astype(BF16)
    gpre, gpost = gpre.reshape(1, d), gpost.reshape(1, d)
    xspec = pl.BlockSpec((None, tm, d), lambda bi, i: (bi, i, 0))
    const = lambda a: pl.BlockSpec(a.shape, lambda bi, i: (0, 0), pipeline_mode=pl.Buffered(1))
    return pl.pallas_call(
        _conv_ffn_kernel,
        grid=(b, s // tm),
        in_specs=[xspec, const(gpre), const(w_in), const(conv_w), const(conv_b), const(w_out), const(gpost)],
        out_specs=xspec,
        out_shape=jax.ShapeDtypeStruct(x.shape, F32),
        scratch_shapes=[pltpu.VMEM((SUBLANES, 2 * D_FF), F32)],
        compiler_params=_cparams("arbitrary", "arbitrary"),
        name="conv_ffn",
    )(x, gpre, w_in, conv_w, conv_b, w_out, gpost)


def _dilated_qkv_kernel(x_ref, g_ref, w_ref, c_ref, s1_ref, s2_ref, o_ref, xn_ref, ybuf_ref, *, dil):
    sec = pl.program_id(2)
    tm = x_ref.shape[0]

    @pl.when(sec == 0)
    def _():
        xn_ref[...] = (_rms(x_ref[...]) * g_ref[...]).astype(BF16)

    y = _dot(xn_ref[...], w_ref[...])
    on = sec < 2
    c = jnp.where(on, c_ref[...], 1.0)
    s1 = jnp.where(on, s1_ref[...], 0.0)
    s2 = jnp.where(on, s2_ref[...], 0.0)
    scale = jnp.where(sec == 0, A_HEAD_DIM ** -0.5, 1.0)
    y = jnp.concatenate([_rope_tile(y[:, t * LANES:(t + 1) * LANES], c, s1, s2, A_ROT // 2)
                         for t in range(A_WIDTH // LANES)], axis=1) * scale
    if dil == 1:
        o_ref[0] = y.astype(BF16)
    else:
        nt = A_WIDTH // LANES
        for t in range(nt):
            ybuf_ref[t] = y[:, t * LANES:(t + 1) * LANES]
        for r in range(dil):
            o_ref[r] = jnp.concatenate(
                [ybuf_ref[t, pl.ds(r, tm // dil, stride=dil), :] for t in range(nt)], axis=1).astype(BF16)


def _dilated_qkv(x, g_pre, w, tabs, dil, tm=1024):
    b, s, d = x.shape
    tm = min(tm, s)
    nblk = s // tm
    tab = pl.BlockSpec((tm, LANES), lambda bi, i, sec: (bi * nblk + i, 0))
    return pl.pallas_call(
        functools.partial(_dilated_qkv_kernel, dil=dil),
        grid=(b, nblk, 3),
        in_specs=[
            pl.BlockSpec((None, tm, d), lambda bi, i, sec: (bi, i, 0)),
            pl.BlockSpec((1, d), lambda bi, i, sec: (0, 0)),
            pl.BlockSpec((d, A_WIDTH), lambda bi, i, sec: (0, sec)),
            tab, tab, tab,
        ],
        out_specs=pl.BlockSpec((None, dil, tm // dil, A_WIDTH), lambda bi, i, sec: (bi, 0, i, sec)),
        out_shape=jax.ShapeDtypeStruct((b, dil, s // dil, 3 * A_WIDTH), BF16),
        scratch_shapes=[pltpu.VMEM((tm, d), BF16), pltpu.VMEM((A_WIDTH // LANES, tm, LANES), F32)],
        compiler_params=_cparams("parallel", "parallel", "arbitrary"),
        name="dilated_qkv",
    )(x, g_pre.reshape(1, d), w, *tabs)


A_STEP_BLOCKS = 4


def _dilated_attn_kernel(q_ref, kc_ref, kp_ref, vc_ref, vp_ref, o_ref, lse_ref):
    n = pl.program_id(2)
    qi = lax.broadcasted_iota(jnp.int32, (A_BLOCK, A_BLOCK), 0)
    kj = lax.broadcasted_iota(jnp.int32, (A_BLOCK, A_BLOCK), 1)
    band = kj >= qi
    mask_c = kj <= qi
    first = lax.broadcasted_iota(jnp.int32, (A_BLOCK, LANES), 1) < A_HEAD_DIM
    zero = jnp.zeros((), BF16)
    nblk = q_ref.shape[0] // A_BLOCK
    nslab = A_WIDTH // LANES
    tiles = [(j, hp) for j in range(nblk) for hp in range(nslab)]
    rows = lambda j: slice(j * A_BLOCK, (j + 1) * A_BLOCK)
    lanes = lambda hp: slice(hp * LANES, (hp + 1) * LANES)
    grab = lambda ref: jnp.stack([ref[rows(j), lanes(hp)] for j, hp in tiles])
    q, kc, vc = grab(q_ref), grab(kc_ref), grab(vc_ref)
    prev_of = lambda cur_ref, prev_ref: jnp.stack(
        [prev_ref[:, lanes(hp)] if j == 0 else cur_ref[rows(j - 1), lanes(hp)] for j, hp in tiles])
    kp, vp = prev_of(kc_ref, kp_ref), prev_of(vc_ref, vp_ref)
    nt_dims, nn_dims = _BMM_DIMS["nt"], _BMM_DIMS["nn"]
    bdot = lambda a, b, dims: lax.dot_general(a, b, dims, preferred_element_type=F32)
    o_pair = lse_pair = None
    for h in range(2):
        sel = first if h == 0 else jnp.logical_not(first)
        qh = jnp.where(sel, q, zero)
        s_p = jnp.where(band, bdot(qh, kp, nt_dims), NEG_INF)
        s_p = jnp.concatenate([jnp.where(n > 0, s_p[:nslab], NEG_INF), s_p[nslab:]], axis=0)
        s_c = jnp.where(mask_c, bdot(qh, kc, nt_dims), NEG_INF)
        m = jnp.max(jnp.maximum(s_p, s_c), axis=-1, keepdims=True)
        p_p, p_c = jnp.exp(s_p - m), jnp.exp(s_c - m)
        l = jnp.sum(p_p + p_c, axis=-1, keepdims=True)
        o = (bdot(p_p.astype(BF16), jnp.where(sel, vp, zero), nn_dims)
             + bdot(p_c.astype(BF16), jnp.where(sel, vc, zero), nn_dims)) / l
        lse = jnp.broadcast_to(m + jnp.log(l), o.shape)
        o_pair = o if h == 0 else o_pair + o
        lse_pair = lse if h == 0 else jnp.where(first, lse_pair, lse)
    for t, (j, hp) in enumerate(tiles):
        o_ref[rows(j), lanes(hp)] = o_pair[t].astype(o_ref.dtype)
        lse_ref[rows(j), lanes(hp)] = lse_pair[t]


def _dilated_attn(qkv):
    b, dil, ln, _ = qkv.shape
    nblk = min(A_STEP_BLOCKS, ln // A_BLOCK)
    tq = nblk * A_BLOCK

    def cur(section):
        return pl.BlockSpec((None, None, tq, A_WIDTH), lambda bi, r, n: (bi, r, n, section))

    def prev(section):
        return pl.BlockSpec((None, None, A_BLOCK, A_WIDTH),
                            lambda bi, r, n: (bi, r, jnp.maximum(n * nblk - 1, 0), section))

    return pl.pallas_call(
        _dilated_attn_kernel,
        grid=(b, dil, ln // tq),
        in_specs=[cur(0), cur(1), prev(1), cur(2), prev(2)],
        out_specs=[cur(0), cur(0)],
        out_shape=[jax.ShapeDtypeStruct((b, dil, ln, A_WIDTH), BF16),
                   jax.ShapeDtypeStruct((b, dil, ln, A_WIDTH), F32)],
        compiler_params=_cparams("parallel", "parallel", "arbitrary"),
        name="dilated_attn",
    )(qkv, qkv, qkv, qkv, qkv)


def _dilated_combine_kernel(*refs, dils):
    ng = len(dils)
    o_refs, l_refs = refs[:ng], refs[ng:2 * ng]
    w_ref, g_ref, res_ref, out_ref = refs[2 * ng:2 * ng + 4]
    bufs = list(refs[2 * ng + 4:])
    tm = res_ref.shape[0]

    def natural(ref, dil):
        if dil == 1:
            return ref[0].astype(F32)
        buf = bufs.pop()
        nt = A_WIDTH // LANES
        for r in range(dil):
            rows = ref[r].astype(F32)
            for t in range(nt):
                buf[t, pl.ds(r, tm // dil, stride=dil), :] = rows[:, t * LANES:(t + 1) * LANES]
        return jnp.concatenate([buf[t] for t in range(nt)], axis=1)

    os_ = [natural(r, d) for r, d in zip(o_refs, dils)]
    ls = [natural(r, d) for r, d in zip(l_refs, dils)]
    m = functools.reduce(jnp.maximum, ls)
    es = [jnp.exp(l - m) for l in ls]
    o = sum(e * o for e, o in zip(es, os_)) / sum(es)
    y = _dot(o.astype(BF16), w_ref[...])
    out_ref[...] = res_ref[...] + _rms(y) * g_ref[...]


def _dilated_combine(os_, lses, w_o, g, res, tm=512):
    b, s, d = res.shape
    tm = min(tm, s)
    dils = tuple(o.shape[1] for o in os_)
    cls = lambda dil: pl.BlockSpec((None, dil, tm // dil, A_WIDTH), lambda bi, i: (bi, 0, i, 0))
    xspec = pl.BlockSpec((None, tm, d), lambda bi, i: (bi, i, 0))
    n_buf = 2 * sum(dil > 1 for dil in dils)
    return pl.pallas_call(
        functools.partial(_dilated_combine_kernel, dils=dils),
        grid=(b, s // tm),
        in_specs=[cls(dil) for dil in dils] * 2 + [pl.BlockSpec(w_o.shape, lambda bi, i: (0, 0)),
                                                   pl.BlockSpec((1, d), lambda bi, i: (0, 0)), xspec],
        out_specs=xspec,
        out_shape=jax.ShapeDtypeStruct(res.shape, F32),
        scratch_shapes=[pltpu.VMEM((A_WIDTH // LANES, tm, LANES), F32)] * n_buf,
        compiler_params=_cparams("parallel", "parallel"),
        name="dilated_combine",
    )(*os_, *lses, w_o, g.reshape(1, d), res)


def _mixer_dilated(x, tabs_a, g_pre, g_post, w_qkv, w_o):
    ngrp = len(A_GROUPS)
    w = w_qkv.astype(BF16).reshape(w_qkv.shape[0], 3, ngrp, A_WIDTH)
    os_, lses = [], []
    for g, (_, dil) in enumerate(A_GROUPS):
        qkv = _dilated_qkv(x, g_pre, w[:, :, g].reshape(-1, 3 * A_WIDTH), tabs_a, dil)
        o, lse = _dilated_attn(qkv)
        os_.append(o)
        lses.append(lse)
    return _dilated_combine(os_, lses, w_o.astype(BF16), g_post, x)


def _gla_kernel(q_ref, k_ref, v_ref, go_ref, glr_ref, w2_ref, bg_ref, on_ref, o_ref, st_ref):
    i = pl.program_id(2)
    tc = q_ref.shape[0]
    c_len = B_CHUNK

    @pl.when(i == 0)
    def _():
        st_ref[...] = jnp.zeros_like(st_ref)

    nc = tc // c_len
    z = _dot(glr_ref[...].astype(BF16), w2_ref[...]) + bg_ref[...]
    log_a = -_softplus(-z) * (1.0 / B_GATE_NORMALIZER)
    gcum = _chunk_cumsum(log_a, c_len)
    g_end = jnp.concatenate(
        [jnp.broadcast_to(gcum[(c + 1) * c_len - 1:(c + 1) * c_len], (c_len, B_DK)) for c in range(nc)], axis=0)
    q = q_ref[...].astype(F32)
    k = k_ref[...].astype(F32)
    chunks = lambda x: x.reshape(nc, c_len, x.shape[1])
    q_dec = chunks((q * (B_DK ** -0.5) * jnp.exp(gcum)).astype(BF16))
    k_inv = chunks((k * jnp.exp(-gcum)).astype(BF16))
    k_end = chunks(k * jnp.exp(g_end - gcum))
    v = chunks(v_ref[...])
    row = lax.broadcasted_iota(jnp.int32, (c_len, c_len), 0)
    col = lax.broadcasted_iota(jnp.int32, (c_len, c_len), 1)
    bdot = lambda a, b, form: lax.dot_general(a, b, _BMM_DIMS[form], preferred_element_type=F32)
    att = jnp.where(col <= row, bdot(q_dec, k_inv, "nt"), 0.0).astype(BF16)
    o = bdot(att, v, "nn")
    kv = bdot(_bt(v.astype(F32)).astype(BF16), k_end.astype(BF16), "nn")
    st = st_ref[...]
    states = []
    for c in range(nc):
        states.append(st.astype(BF16))
        st = st * jnp.exp(gcum[(c + 1) * c_len - 1:(c + 1) * c_len]) + kv[c]
    st_ref[...] = st
    o = (o + bdot(q_dec, jnp.stack(states), "nt")).reshape(tc, B_DV)
    go = go_ref[...].astype(F32)
    o_ref[...] = (_rms(o) * on_ref[...] * (go * _sigmoid(go))).astype(o_ref.dtype)


def _gla(proj, glr, w_gate2, b_gate, o_norm, tc=512):
    b, s, _ = proj.shape
    tc = min(tc, s)
    nkb = B_KEY_DIM // B_DK
    nvb = 2 * B_KEY_DIM // B_DV
    ngb = nvb + B_VAL_DIM // B_DV
    return pl.pallas_call(
        _gla_kernel,
        grid=(b, B_HEADS, s // tc),
        in_specs=[
            pl.BlockSpec((None, tc, B_DK), lambda bi, h, i: (bi, i, h)),
            pl.BlockSpec((None, tc, B_DK), lambda bi, h, i: (bi, i, nkb + h)),
            pl.BlockSpec((None, tc, B_DV), lambda bi, h, i: (bi, i, nvb + h)),
            pl.BlockSpec((None, tc, B_DV), lambda bi, h, i: (bi, i, ngb + h)),
            pl.BlockSpec((None, tc, LANES), lambda bi, h, i: (bi, i, 0)),
            pl.BlockSpec((LANES, B_DK), lambda bi, h, i: (0, h)),
            pl.BlockSpec((1, B_DK), lambda bi, h, i: (0, h)),
            pl.BlockSpec((1, B_DV), lambda bi, h, i: (0, 0)),
        ],
        out_specs=pl.BlockSpec((None, tc, B_DV), lambda bi, h, i: (bi, i, h)),
        out_shape=jax.ShapeDtypeStruct((b, s, B_VAL_DIM), BF16),
        scratch_shapes=[pltpu.VMEM((B_DV, B_DK), F32)],
        compiler_params=_cparams("parallel", "parallel", "arbitrary"),
        name="gla",
    )(proj, proj, proj, proj, glr, w_gate2, b_gate, o_norm)


def _mixer_gla(x, g_pre, g_post, w_in, w_gate2, b_gate, o_norm, w_o):
    b, s, d = x.shape
    xf = x.reshape(b * s, d)
    main = 2 * B_KEY_DIM + 2 * B_VAL_DIM
    w_main = w_in[:, :main].astype(BF16)
    w_lr = jnp.pad(w_in[:, main:], ((0, 0), (0, LANES - B_GATE_RANK))).astype(BF16)
    proj = _norm_matmul(xf, g_pre, w_main, out_dtype=BF16, tn=main // 4)
    glr = _norm_matmul(xf, g_pre, w_lr, out_dtype=F32, tn=LANES)
    w2 = jnp.pad(w_gate2, ((0, LANES - B_GATE_RANK), (0, 0))).astype(BF16)
    og = _gla(proj.reshape(b, s, main), glr.reshape(b, s, LANES), w2, b_gate.reshape(1, B_KEY_DIM),
              o_norm.reshape(1, B_DV))
    return _matmul_norm_res(og.reshape(b * s, B_VAL_DIM), w_o.astype(BF16), g_post, xf).reshape(b, s, d)


def _mla_proj_kernel(c_ref, qn_ref, kvn_ref, wq_ref, wk_ref, wv_ref, cc_ref, s1_ref, s2_ref,
                     q_ref, k_ref, v_ref):
    c = c_ref[...]
    ckv = c[:, :C_KV_RANK]
    kpe = c[:, C_KV_RANK:C_KV_RANK + C_SLOT]
    cq = c[:, C_KV_RANK + C_SLOT:]
    cc, s1, s2 = cc_ref[...], s1_ref[...], s2_ref[...]
    shift = C_ROPE // 2
    scale = (C_NOPE + C_ROPE) ** -0.5
    q = _dot((_rms(cq) * qn_ref[...]).astype(BF16), wq_ref[...])
    ckvn = (_rms(ckv) * kvn_ref[...]).astype(BF16)
    kn = _dot(ckvn, wk_ref[...])
    kpe = _rope_tile(kpe, cc, s1, s2, shift)
    for h in range(C_HEADS):
        sl = slice(h * C_SLOT, (h + 1) * C_SLOT)
        q_ref[:, sl] = (_rope_tile(q[:, sl], cc, s1, s2, shift) * scale).astype(BF16)
        k_ref[:, sl] = (kn[:, sl] + kpe).astype(BF16)
    v_ref[...] = _dot(ckvn, wv_ref[...]).astype(BF16)


def _mla_proj(c, q_norm, kv_norm, wq, wk, wv, tabs, tm=512):
    t = c.shape[0]
    tm = min(tm, t)
    row = lambda width: pl.BlockSpec((tm, width), lambda i: (i, 0))
    full = lambda a: pl.BlockSpec(a.shape, lambda i: (0, 0))
    qn, kvn = q_norm.reshape(1, C_Q_RANK), kv_norm.reshape(1, C_KV_RANK)
    hw = C_HEADS * C_SLOT
    return pl.pallas_call(
        _mla_proj_kernel,
        grid=(t // tm,),
        in_specs=[row(c.shape[1]), full(qn), full(kvn), full(wq), full(wk), full(wv)] + [row(LANES)] * 3,
        out_specs=[row(hw), row(hw), row(C_HEADS * C_VDIM)],
        out_shape=[jax.ShapeDtypeStruct((t, hw), BF16), jax.ShapeDtypeStruct((t, hw), BF16),
                   jax.ShapeDtypeStruct((t, C_HEADS * C_VDIM), BF16)],
        compiler_params=_cparams("parallel"),
        name="mla_proj",
    )(c, qn, kvn, wq, wk, wv, *tabs)


def _mla_flash_kernel(q_ref, k_ref, v_ref, o_ref, m_ref, l_ref, acc_ref, *, tk):
    i = pl.program_id(2)
    tq = q_ref.shape[0]
    assert tq == tk
    nt = tk // LANES
    m_ref[...] = jnp.full_like(m_ref, NEG_INF)
    l_ref[...] = jnp.zeros_like(l_ref)
    acc_ref[...] = jnp.zeros_like(acc_ref)

    def lane_tiles(x):
        return [x[:, t * LANES:(t + 1) * LANES] for t in range(nt)]

    def step(j, masked):
        start = pl.multiple_of(j * tk, tk)
        v = v_ref[pl.ds(start, tk), :]
        for h in range(2):
            s = _dot_t(q_ref[:, h * C_SLOT:(h + 1) * C_SLOT], k_ref[pl.ds(start, tk), h * C_SLOT:(h + 1) * C_SLOT])
            if masked:
                qi = lax.broadcasted_iota(jnp.int32, (tq, tk), 0)
                kj = lax.broadcasted_iota(jnp.int32, (tq, tk), 1)
                s = jnp.where(kj <= qi, s, NEG_INF)
            tile_max = functools.reduce(jnp.maximum, lane_tiles(s))
            m_old = m_ref[h]
            m_new = jnp.maximum(m_old, jnp.broadcast_to(jnp.max(tile_max, axis=-1, keepdims=True), (tq, LANES)))
            alpha = jnp.exp(m_old - m_new)
            p = jnp.exp(s - jnp.tile(m_new, (1, nt)))
            l_ref[h] = alpha * l_ref[h] + functools.reduce(jnp.add, lane_tiles(p))
            acc_ref[h] = alpha * acc_ref[h] + _dot(p.astype(BF16), v)
            m_ref[h] = m_new

    def body(jj, carry):
        step(2 * jj, False)
        step(2 * jj + 1, False)
        return carry

    lax.fori_loop(0, i // 2, body, 0)

    @pl.when(i % 2 == 1)
    def _():
        step(i - 1, False)

    step(i, True)
    lane = lax.broadcasted_iota(jnp.int32, (tq, 2 * C_VDIM), 1)
    res = [acc_ref[h] / jnp.sum(l_ref[h], axis=-1, keepdims=True) for h in range(2)]
    o_ref[...] = jnp.where(lane < C_VDIM, res[0], res[1]).astype(o_ref.dtype)


def _mla_flash(q, k, v, tq=1024):
    b, s, _ = q.shape
    tq = min(tq, s)
    pairs = C_HEADS // 2
    return pl.pallas_call(
        functools.partial(_mla_flash_kernel, tk=tq),
        grid=(b, pairs, s // tq),
        in_specs=[
            pl.BlockSpec((None, tq, 2 * C_SLOT), lambda bi, p, i: (bi, i, p)),
            pl.BlockSpec((None, s, 2 * C_SLOT), lambda bi, p, i: (bi, 0, p)),
            pl.BlockSpec((None, s, 2 * C_VDIM), lambda bi, p, i: (bi, 0, p)),
        ],
        out_specs=pl.BlockSpec((None, tq, 2 * C_VDIM), lambda bi, p, i: (bi, i, p)),
        out_shape=jax.ShapeDtypeStruct((b, s, C_HEADS * C_VDIM), BF16),
        scratch_shapes=[pltpu.VMEM((2, tq, LANES), F32)] * 3,
        compiler_params=_cparams("parallel", "parallel", "arbitrary"),
        name="mla_flash",
    )(q, k, v)


def _mixer_mla(x, tabs_c, g_pre, g_post, w_in, q_norm, w_uq, kv_norm, w_ukv, w_o):
    b, s, d = x.shape
    xf = x.reshape(b * s, d)
    kpe_w = jnp.pad(w_in[:, C_Q_RANK + C_KV_RANK:], ((0, 0), (C_NOPE, C_SLOT - C_NOPE - C_ROPE)))
    w_c = jnp.concatenate([w_in[:, C_Q_RANK:C_Q_RANK + C_KV_RANK], kpe_w, w_in[:, :C_Q_RANK]], axis=1)
    c = _norm_matmul(xf, g_pre, w_c.astype(BF16), out_dtype=F32, tn=w_c.shape[1])
    pad = C_SLOT - C_NOPE - C_ROPE
    wq = jnp.pad(w_uq.reshape(C_Q_RANK, C_HEADS, C_NOPE + C_ROPE), ((0, 0), (0, 0), (0, pad)))
    wkv = w_ukv.reshape(C_KV_RANK, C_HEADS, C_NOPE + C_VDIM)
    wk = jnp.pad(wkv[:, :, :C_NOPE], ((0, 0), (0, 0), (0, C_SLOT - C_NOPE)))
    wv = wkv[:, :, C_NOPE:]
    q, k, v = _mla_proj(c, q_norm, kv_norm, wq.reshape(C_Q_RANK, -1).astype(BF16),
                        wk.reshape(C_KV_RANK, -1).astype(BF16), wv.reshape(C_KV_RANK, -1).astype(BF16), tabs_c)
    hw = C_HEADS * C_SLOT
    o = _mla_flash(q.reshape(b, s, hw), k.reshape(b, s, hw), v.reshape(b, s, C_HEADS * C_VDIM))
    return _matmul_norm_res(o.reshape(b * s, C_HEADS * C_VDIM), w_o.astype(BF16), g_post, xf).reshape(b, s, d)


def _rwkv_proj_kernel(x_ref, gpre_ref, mix_ref, wr_ref, wk_ref, wv_ref, w1_ref, w2_ref, a1_ref, a2_ref,
                      g1_ref, g2_ref, w0_ref, a0_ref,
                      r_ref, k_ref, v_ref, lw_ref, a_ref, g_ref, hbuf_ref, carry_ref):
    i = pl.program_id(1)
    tm = x_ref.shape[0]
    hn = _rms(x_ref[...]) * gpre_ref[...]
    @pl.when(i == 0)
    def _():
        hbuf_ref[0:SUBLANES, :] = jnp.zeros((SUBLANES, hn.shape[1]), F32)

    @pl.when(i > 0)
    def _():
        hbuf_ref[0:SUBLANES, :] = carry_ref[...]

    hbuf_ref[SUBLANES:SUBLANES + tm, :] = hn
    carry_ref[...] = hn[tm - SUBLANES:, :]
    xx = hbuf_ref[SUBLANES - 1:SUBLANES - 1 + tm, :] - hn
    mix = mix_ref[...]
    mixed = lambda n: (hn + xx * mix[n:n + 1]).astype(BF16)
    r_ref[...] = _dot(mixed(0), wr_ref[...])
    k_ref[...] = _dot(mixed(2), wk_ref[...])
    v_ref[...] = _dot(mixed(3), wv_ref[...])
    wl = _dot(jnp.tanh(_dot(mixed(1), w1_ref[...])).astype(BF16), w2_ref[...])
    w = -_softplus(-(w0_ref[...] + wl)) - 0.5
    lw_ref[...] = -jnp.exp(w)
    al = _dot(_dot(mixed(4), a1_ref[...]).astype(BF16), a2_ref[...])
    a_ref[...] = _sigmoid(a0_ref[...] + al)
    g_ref[...] = _dot(_sigmoid(_dot(mixed(5), g1_ref[...])).astype(BF16), g2_ref[...]).astype(g_ref.dtype)


def _rwkv_proj(x, gpre, mix, w_rkv, w1, w2, a1, a2, g1, g2, w0, a0, tm=512):
    b, s, d = x.shape
    tm = min(tm, s)
    const = lambda a: pl.BlockSpec(a.shape, lambda bi, i: (0,) * a.ndim)
    xspec = pl.BlockSpec((None, tm, d), lambda bi, i: (bi, i, 0))
    bf = lambda a: a.astype(BF16)
    args = [gpre.reshape(1, d), mix, bf(w_rkv[0]), bf(w_rkv[1]), bf(w_rkv[2]), bf(w1), bf(w2), bf(a1), bf(a2),
            bf(g1), bf(g2), w0.reshape(1, d), a0.reshape(1, d)]
    f32_out = jax.ShapeDtypeStruct(x.shape, F32)
    return pl.pallas_call(
        _rwkv_proj_kernel,
        grid=(b, s // tm),
        in_specs=[xspec] + [const(a) for a in args],
        out_specs=[xspec] * 6,
        out_shape=[f32_out] * 5 + [jax.ShapeDtypeStruct(x.shape, BF16)],
        scratch_shapes=[pltpu.VMEM((tm + SUBLANES, d), F32), pltpu.VMEM((SUBLANES, d), F32)],
        compiler_params=_cparams("arbitrary", "arbitrary"),
        name="rwkv_proj",
    )(x, *args)


def _split(x):
    hi = x.astype(BF16)
    return hi, (x - hi.astype(F32)).astype(BF16)


_BMM_DIMS = {
    "nn": (((2,), (1,)), ((0,), (0,))),
    "nt": (((2,), (2,)), ((0,), (0,))),
}


def _bmm(a, b, passes, form="nn"):
    dims = _BMM_DIMS[form]
    dg = functools.partial(lax.dot_general, dimension_numbers=dims, preferred_element_type=F32)
    if passes == 6:
        return dg(a, b, precision=HIGHEST)
    if passes == 1:
        return dg(a.astype(BF16), b.astype(BF16))
    a_hi, a_lo = _split(a)
    b_hi, b_lo = _split(b)
    return dg(a_hi, b_hi) + (dg(a_hi, b_lo) + dg(a_lo, b_hi))


def _bt(x):
    return jnp.swapaxes(x, 1, 2)


RWKV_PASSES = dict(score=1, inv_low=3, inv_high=1, apply=1, state=1, seq=1)


def _rwkv_rec_kernel(r_ref, k_ref, v_ref, lw_ref, a_ref, kk_ref, ka_ref, rk_ref, lnw_ref, lnb_ref,
                     o_ref, st_ref):
    i = pl.program_id(2)
    tc = r_ref.shape[0]
    ln, dh = D_CHUNK, D_HEAD
    nc = tc // ln
    ps = RWKV_PASSES

    @pl.when(i == 0)
    def _():
        st_ref[...] = jnp.zeros_like(st_ref)

    lw = lw_ref[...]
    gc = _chunk_cumsum(lw, ln)
    g_end = jnp.concatenate(
        [jnp.broadcast_to(gc[(c + 1) * ln - 1:(c + 1) * ln], (ln, LANES)) for c in range(nc)], axis=0)

    lane = lax.broadcasted_iota(jnp.int32, (tc, LANES), 1)
    first = lane < dh

    def head_sum(x):
        s0 = jnp.sum(jnp.where(first, x, 0.0), axis=-1, keepdims=True)
        s1 = jnp.sum(jnp.where(first, 0.0, x), axis=-1, keepdims=True)
        return jnp.where(first, s0, s1)

    r, k, v, a = r_ref[...], k_ref[...], v_ref[...], a_ref[...]
    kk = k * kk_ref[...]
    kk = kk / jnp.maximum(jnp.sqrt(head_sum(kk * kk)), 1e-12)
    k_mod = k * (1.0 + (a - 1.0) * ka_ref[...])
    beta = kk * a
    e_neg = jnp.exp(-gc)
    e_end = jnp.exp(g_end - gc)
    a_t = -kk * jnp.exp(gc - lw)
    r_t = r * jnp.exp(gc)

    first_c = lax.broadcasted_iota(jnp.int32, (ln, LANES), 1) < dh

    def pair(x):
        return jnp.stack([jnp.concatenate([jnp.where(first_c, x[c * ln:(c + 1) * ln], 0.0),
                                           jnp.where(first_c, 0.0, x[c * ln:(c + 1) * ln])], axis=0)
                          for c in range(nc)])

    pl_ = 2 * ln
    a_b, r_b, v_b = pair(a_t), pair(r_t), pair(v)
    bend_b, kend_b = pair(beta * e_end), pair(k_mod * e_end)
    sc = _bmm(jnp.concatenate([a_b, r_b], axis=1),
              jnp.concatenate([pair(beta * e_neg), pair(k_mod * e_neg)], axis=1), ps["score"], "nt")
    row = lax.broadcasted_iota(jnp.int32, (pl_, pl_), 0)
    col = lax.broadcasted_iota(jnp.int32, (pl_, pl_), 1)
    incl, strict = (col % ln) <= (row % ln), (col % ln) < (row % ln)
    eye = (row == col).astype(F32)
    n_ab = jnp.where(strict, sc[:, :pl_, :pl_], 0.0)
    a_ak = jnp.where(strict, sc[:, :pl_, pl_:], 0.0)
    a_rb = jnp.where(incl, sc[:, pl_:, :pl_], 0.0)
    a_rk = jnp.where(incl, sc[:, pl_:, pl_:], 0.0)
    n_d = jnp.where((row // D_SUB) == (col // D_SUB), n_ab, 0.0)
    n_o = n_ab - n_d
    p = eye + n_d
    pw = n_d
    n_sq = int(math.log2(D_SUB)) - 1
    for t in range(n_sq):
        passes = ps["inv_low"] if t < n_sq - 1 else ps["inv_high"]
        pw = _bmm(pw, pw, passes)
        p = p + _bmm(p, pw, passes)
    m1 = _bmm(p, n_o, ps["inv_high"])
    m2 = _bmm(m1, m1, ps["inv_high"])
    assert D_CHUNK // D_SUB == 4
    av = _bmm(a_ak, v_b, ps["apply"])
    w12 = _bmm(p, jnp.concatenate([a_b, av], axis=2), ps["apply"])
    w12 = w12 + _bmm(m1, w12, ps["apply"])
    w12 = w12 + _bmm(m2, w12, ps["apply"])
    qy = _bmm(a_rb, w12, ps["apply"])
    q_eff = r_b + qy[:, :, :LANES]
    y0 = qy[:, :, LANES:] + _bmm(a_rk, v_b, ps["apply"])
    w1, w2 = w12[:, :, :LANES], w12[:, :, LANES:]
    decay = jnp.stack([jnp.exp(gc[(c + 1) * ln - 1:(c + 1) * ln]) for c in range(nc)])
    trans = eye * decay + _bmm(_bt(w1), bend_b, ps["state"])
    s_add = _bmm(_bt(jnp.concatenate([w2, v_b], axis=1)), jnp.concatenate([bend_b, kend_b], axis=1), ps["state"])

    st = st_ref[...][None]
    ys = []
    for c in range(nc):
        yp = _bmm(q_eff[c:c + 1], st, ps["seq"], "nt")[0] + y0[c]
        ys.append(yp[:ln] + yp[ln:])
        st = _bmm(st, trans[c:c + 1], ps["seq"]) + s_add[c:c + 1]
    st_ref[...] = st[0]
    y = jnp.concatenate(ys, axis=0)

    mu = head_sum(y) * (1.0 / dh)
    yc = y - mu
    var = head_sum(yc * yc) * (1.0 / dh)
    gn = yc * lax.rsqrt(var + D_GN_EPS) * lnw_ref[...] + lnb_ref[...]
    bonus = head_sum(r * k_mod * rk_ref[...]) * v
    o_ref[...] = gn + bonus


def _rwkv_rec(r, k, v, lw, a, k_k, k_a, r_k, lnx_w, lnx_b, tc=512):
    b, s, d = r.shape
    tc = min(tc, s)
    blk = pl.BlockSpec((None, tc, LANES), lambda bi, p, i: (bi, i, p))
    par = pl.BlockSpec((1, LANES), lambda bi, p, i: (0, p))
    params = [t.reshape(1, d) for t in (k_k, k_a, r_k, lnx_w, lnx_b)]
    return pl.pallas_call(
        _rwkv_rec_kernel,
        grid=(b, d // LANES, s // tc),
        in_specs=[blk] * 5 + [par] * 5,
        out_specs=blk,
        out_shape=jax.ShapeDtypeStruct(r.shape, F32),
        scratch_shapes=[pltpu.VMEM((LANES, LANES), F32)],
        compiler_params=_cparams("parallel", "parallel", "arbitrary"),
        name="rwkv_rec",
    )(r, k, v, lw, a, *params)


def _mixer_rwkv7(x, g_pre, g_post, mix, w_rkv, w0, w1, w2, a0, a1, a2, g1, g2, k_k, k_a, r_k, lnx_w, lnx_b, w_o):
    b, s, d = x.shape
    r, k, v, lw, a, g = _rwkv_proj(x, g_pre, mix, w_rkv, w1, w2, a1, a2, g1, g2, w0, a0)
    y = _rwkv_rec(r, k, v, lw, a, k_k, k_a, r_k, lnx_w, lnx_b)
    flat = lambda t: t.reshape(b * s, d)
    return _matmul_norm_res(flat(y), w_o.astype(BF16), g_post, flat(x), gate=flat(g)).reshape(b, s, d)


def kernel(x, mem, positions, ln_gains, mem_norm, mem_w_kv, mem_w_q, mem_w_o, ffn_w_in, ffn_conv_w, ffn_conv_b, ffn_w_out, a_w_qkv, a_w_o, b_w_in, b_w_gate2, b_gate_bias, b_o_norm, b_w_o, c_w_in, c_q_norm, c_w_uq, c_kv_norm, c_w_ukv, c_w_o, d_mix, d_w_rkv, d_w0, d_w1, d_w2, d_a0, d_a1, d_a2, d_g1, d_g2, d_k_k, d_k_a, d_r_k, d_lnx_w, d_lnx_b, d_w_o):
    b, s, d = x.shape
    depth = ln_gains.shape[0]
    mlen = mem.shape[1]
    mwidth = M_HEADS * M_HEAD_DIM
    mkv = _norm_matmul(mem.reshape(b * mlen, d), mem_norm, mem_w_kv.astype(BF16), out_dtype=BF16,
                       tn=2 * mwidth, tm=mlen)
    mkv = mkv.reshape(b, mlen, 2 * mwidth)
    mem_k, mem_v = mkv[:, :, :mwidth], mkv[:, :, mwidth:]
    tabs = _rope_tables(positions)
    tabs_a, tabs_c = tabs[:3], tabs[3:]
    for i in range(depth):
        m, j = i % N_MIXERS, i // N_MIXERS
        gains = ln_gains[i]
        if m == 0:
            x = _mixer_dilated(x, tabs_a, gains[0], gains[1], a_w_qkv[j], a_w_o[j])
        elif m == 1:
            x = _mixer_gla(x, gains[0], gains[1], b_w_in[j], b_w_gate2[j], b_gate_bias[j], b_o_norm[j], b_w_o[j])
        elif m == 2:
            x = _mixer_mla(x, tabs_c, gains[0], gains[1], c_w_in[j], c_q_norm[j], c_w_uq[j], c_kv_norm[j],
                           c_w_ukv[j], c_w_o[j])
        else:
            x = _mixer_rwkv7(x, gains[0], gains[1], d_mix[j], d_w_rkv[j], d_w0[j], d_w1[j], d_w2[j], d_a0[j],
                             d_a1[j], d_a2[j], d_g1[j], d_g2[j], d_k_k[j], d_k_a[j], d_r_k[j], d_lnx_w[j],
                             d_lnx_b[j], d_w_o[j])
        x = _mem_attn(x, gains[2], mem_w_q[i].astype(BF16), mem_k, mem_v, mem_w_o[i].astype(BF16), gains[3])
        x = _conv_ffn(x, gains[4], ffn_w_in[i], ffn_conv_w[i], ffn_conv_b[i], ffn_w_out[i], gains[5])
    return x
```

```python
import functools
import math

import jax
import jax.numpy as jnp
from jax import lax
from jax.experimental import pallas as pl
from jax.experimental.pallas import tpu as pltpu

F32 = jnp.float32
BF16 = jnp.bfloat16
HIGHEST = lax.Precision.HIGHEST

D_MODEL = 1024
N_MIXERS = 4
NORM_EPS = 1e-6
ROPE_THETA = 500000.0
NEG_INF = -1e30

A_HEAD_DIM = 64
A_HEADS = 8
A_GROUPS = ((128, 1), (512, 4), (2048, 16))
A_BLOCK = 128
A_ROT = A_HEAD_DIM // 4
A_WIDTH = A_HEADS * A_HEAD_DIM

B_HEADS = 4
B_KEY_DIM = 512
B_VAL_DIM = 1024
B_DK = B_KEY_DIM // B_HEADS
B_DV = B_VAL_DIM // B_HEADS
B_GATE_RANK = 16
B_GATE_NORMALIZER = 16.0
B_CHUNK = 64

C_HEADS = 16
C_Q_RANK = 384
C_KV_RANK = 256
C_NOPE = 64
C_ROPE = 32
C_VDIM = 64
C_SLOT = 128

D_HEAD = 64
D_HEADS = D_MODEL // D_HEAD
D_GN_EPS = 64e-5
D_CHUNK = 64
D_SUB = 16

M_HEADS = 4
M_HEAD_DIM = 128

D_FF = 2816
CONV_WIDTH = 3
FF_CHUNK = 256

LANES = 128
SUBLANES = 8
VMEM_LIMIT = 56 * 1024 * 1024


def _cparams(*sem):
    return pltpu.CompilerParams(dimension_semantics=sem, vmem_limit_bytes=VMEM_LIMIT)


def _rms(x, eps=NORM_EPS):
    return x * lax.rsqrt(jnp.mean(x * x, axis=-1, keepdims=True) + eps)


def _sigmoid(x):
    return 1.0 / (1.0 + jnp.exp(-x))


def _softplus(x):
    return jnp.maximum(x, 0.0) + jnp.log(1.0 + jnp.exp(-jnp.abs(x)))


def _dot(a, b, **kw):
    return jnp.dot(a, b, preferred_element_type=F32, **kw)


def _dot_t(a, b, **kw):
    return lax.dot_general(a, b, (((1,), (1,)), ((), ())), preferred_element_type=F32, **kw)


def _tdot(a, b, **kw):
    return lax.dot_general(a, b, (((0,), (0,)), ((), ())), preferred_element_type=F32, **kw)


def _chunk_cumsum(x, ln):
    rows, cols = x.shape
    r = lax.broadcasted_iota(jnp.int32, (rows, rows), 0)
    c = lax.broadcasted_iota(jnp.int32, (rows, rows), 1)
    tri = ((r // ln == c // ln) & (c <= r)).astype(BF16)
    hi = x.astype(BF16)
    rem = x - hi.astype(F32)
    mid = rem.astype(BF16)
    lo = (rem - mid.astype(F32)).astype(BF16)
    g3 = _dot(tri, jnp.concatenate([hi, mid, lo], axis=1))
    return g3[:, :cols] + (g3[:, cols:2 * cols] + g3[:, 2 * cols:])


def _rope_tile(y, c, s1, s2, shift):
    return y * c + pltpu.roll(y, LANES - shift, 1) * s1 + pltpu.roll(y, shift, 1) * s2


def _rope_table_kernel(pos_ref, ca_ref, sa1_ref, sa2_ref, cc_ref, sc1_ref, sc2_ref):
    pos = pos_ref[...].astype(F32)
    lane = lax.broadcasted_iota(jnp.int32, (1, LANES), 1)
    log_theta = math.log(ROPE_THETA)

    def tables(rel, half, rot):
        in_span = (rel >= 0) & (rel < rot)
        idx = jnp.where(in_span, jnp.where(rel < half, rel, rel - half), 0).astype(F32)
        inv_freq = jnp.exp(-(idx * (2.0 / rot)) * log_theta)
        ang = pos * inv_freq
        cos, sin = jnp.cos(ang), jnp.sin(ang)
        first = (rel >= 0) & (rel < half)
        second = (rel >= half) & (rel < rot)
        return (jnp.where(in_span, cos, 1.0), jnp.where(first, -sin, 0.0), jnp.where(second, sin, 0.0))

    ca, sa1, sa2 = tables(lane % A_HEAD_DIM, A_ROT // 2, A_ROT)
    cc, sc1, sc2 = tables(lane - C_NOPE, C_ROPE // 2, C_ROPE)
    ca_ref[...], sa1_ref[...], sa2_ref[...] = ca, sa1, sa2
    cc_ref[...], sc1_ref[...], sc2_ref[...] = cc, sc1, sc2


def _rope_tables(positions):
    t = positions.size
    tm = min(t, 1024)
    pos = positions.reshape(t, 1)
    out = jax.ShapeDtypeStruct((t, LANES), F32)
    spec = pl.BlockSpec((tm, LANES), lambda i: (i, 0))
    return pl.pallas_call(
        _rope_table_kernel,
        grid=(t // tm,),
        in_specs=[pl.BlockSpec((tm, 1), lambda i: (i, 0))],
        out_specs=[spec] * 6,
        out_shape=[out] * 6,
        compiler_params=_cparams("parallel"),
        name="rope_tables",
    )(pos)


def _norm_matmul_kernel(x_ref, g_ref, w_ref, o_ref, xn_ref):
    @pl.when(pl.program_id(1) == 0)
    def _():
        xn_ref[...] = (_rms(x_ref[...]) * g_ref[...]).astype(BF16)

    o_ref[...] = _dot(xn_ref[...], w_ref[...]).astype(o_ref.dtype)


def _norm_matmul(x, g, w, *, out_dtype, tn, tm=1024):
    t, k = x.shape
    n = w.shape[1]
    tm = min(tm, t)
    return pl.pallas_call(
        _norm_matmul_kernel,
        grid=(t // tm, n // tn),
        in_specs=[
            pl.BlockSpec((tm, k), lambda i, j: (i, 0)),
            pl.BlockSpec((1, k), lambda i, j: (0, 0)),
            pl.BlockSpec((k, tn), lambda i, j: (0, j)),
        ],
        out_specs=pl.BlockSpec((tm, tn), lambda i, j: (i, j)),
        out_shape=jax.ShapeDtypeStruct((t, n), out_dtype),
        scratch_shapes=[pltpu.VMEM((tm, k), BF16)],
        compiler_params=_cparams("parallel", "arbitrary"),
        name="norm_matmul",
    )(x, g.reshape(1, k), w)


def _matmul_norm_res_kernel(*refs, gated):
    if gated:
        h_ref, gate_ref, w_ref, g_ref, res_ref, o_ref = refs
        h = (h_ref[...] * gate_ref[...].astype(F32)).astype(BF16)
    else:
        h_ref, w_ref, g_ref, res_ref, o_ref = refs
        h = h_ref[...]
    y = _dot(h, w_ref[...])
    o_ref[...] = res_ref[...] + _rms(y) * g_ref[...]


def _matmul_norm_res(h, w, g, res, gate=None, tm=512):
    t, k = h.shape
    n = w.shape[1]
    tm = min(tm, t)
    row = lambda width: pl.BlockSpec((tm, width), lambda i: (i, 0))
    full = lambda a: pl.BlockSpec(a.shape, lambda i: (0, 0))
    g2 = g.reshape(1, n)
    if gate is None:
        args, specs = [h, w, g2, res], [row(k), full(w), full(g2), row(n)]
    else:
        args, specs = [h, gate, w, g2, res], [row(k), row(k), full(w), full(g2), row(n)]
    return pl.pallas_call(
        functools.partial(_matmul_norm_res_kernel, gated=gate is not None),
        grid=(t // tm,),
        in_specs=specs,
        out_specs=row(n),
        out_shape=jax.ShapeDtypeStruct((t, n), F32),
        compiler_params=_cparams("parallel"),
        name="matmul_norm_res",
    )(*args)


def _mem_attn_kernel(x_ref, gpre_ref, wq_ref, k_ref, v_ref, wo_ref, gpost_ref, o_ref):
    x = x_ref[...]
    hn = (_rms(x) * gpre_ref[...]).astype(BF16)
    q = (_dot(hn, wq_ref[...]) * (M_HEAD_DIM ** -0.5)).astype(BF16)
    k, v = k_ref[...], v_ref[...]
    heads = lambda t: jnp.stack([t[:, h * M_HEAD_DIM:(h + 1) * M_HEAD_DIM] for h in range(M_HEADS)])
    s = lax.dot_general(heads(q), heads(k), _BMM_DIMS["nt"], preferred_element_type=F32)
    p = jnp.exp(s - jnp.max(s, axis=-1, keepdims=True))
    p = p / jnp.sum(p, axis=-1, keepdims=True)
    o = lax.dot_general(p.astype(BF16), heads(v), _BMM_DIMS["nn"], preferred_element_type=F32)
    o = jnp.concatenate([o[h] for h in range(M_HEADS)], axis=1).astype(BF16)
    y = _dot(o, wo_ref[...])
    o_ref[...] = x + _rms(y) * gpost_ref[...]


def _mem_attn(x, gpre, wq, mem_k, mem_v, wo, gpost, tm=512):
    b, s, d = x.shape
    tm = min(tm, s)
    width = M_HEADS * M_HEAD_DIM
    mlen = mem_k.shape[1]
    const = lambda a: pl.BlockSpec(a.shape, lambda bi, i: (0,) * a.ndim)
    gpre, gpost = gpre.reshape(1, d), gpost.reshape(1, d)
    xspec = pl.BlockSpec((None, tm, d), lambda bi, i: (bi, i, 0))
    mspec = pl.BlockSpec((None, mlen, width), lambda bi, i: (bi, 0, 0))
    return pl.pallas_call(
        _mem_attn_kernel,
        grid=(b, s // tm),
        in_specs=[xspec, const(gpre), const(wq), mspec, mspec, const(wo), const(gpost)],
        out_specs=xspec,
        out_shape=jax.ShapeDtypeStruct(x.shape, F32),
        compiler_params=_cparams("parallel", "parallel"),
        name="mem_attn",
    )(x, gpre, wq, mem_k, mem_v, wo, gpost)


def _conv_ffn_kernel(x_ref, gpre_ref, win_ref, cw_ref, cb_ref, wout_ref, gpost_ref, o_ref, carry_ref):
    i = pl.program_id(1)
    tm = x_ref.shape[0]
    fc = FF_CHUNK

    @pl.when(i == 0)
    def _():
        carry_ref[...] = jnp.zeros_like(carry_ref)

    x = x_ref[...]
    hn = (_rms(x) * gpre_ref[...]).astype(BF16)
    acc = jnp.zeros((tm, x.shape[1]), F32)
    nc = D_FF // fc
    gate_cols = lambda j: slice(j * fc, (j + 1) * fc)
    val_cols = lambda j: slice(D_FF + j * fc, D_FF + (j + 1) * fc)
    us = [(_dot(hn, win_ref[:, gate_cols(j)]), _dot(hn, win_ref[:, val_cols(j)])) for j in range(nc)]

    def conv(cols, u):
        ext = jnp.concatenate([carry_ref[:, cols], u], axis=0)
        carry_ref[:, cols] = u[tm - SUBLANES:, :]
        cw = cw_ref[:, cols]
        return (cb_ref[:, cols] + ext[SUBLANES - 2:SUBLANES - 2 + tm] * cw[0:1]
                + ext[SUBLANES - 1:SUBLANES - 1 + tm] * cw[1:2] + u * cw[2:3])

    for j in range(nc):
        gate, val = conv(gate_cols(j), us[j][0]), conv(val_cols(j), us[j][1])
        h = (gate * _sigmoid(gate) * val).astype(BF16)
        acc = acc + _dot(h, wout_ref[j * fc:(j + 1) * fc, :])
    o_ref[...] = x + _rms(acc) * gpost_ref[...]


def _conv_ffn(x, gpre, w_in, conv_w, conv_b, w_out, gpost, tm=512):
    b, s, d = x.shape
    tm = min(tm, s)
    w_in = w_in.astype(BF16)
    conv_b = conv_b.reshape(1, 2 * D_FF)
    w_out = w_out.astype(BF16)
    gpre, gpost = gpre.reshape(1, d), gpost.reshape(1, d)
    xspec = pl.BlockSpec((None, tm, d), lambda bi, i: (bi, i, 0))
    const = lambda a: pl.BlockSpec(a.shape, lambda bi, i: (0, 0), pipeline_mode=pl.Buffered(1))
    return pl.pallas_call(
        _conv_ffn_kernel,
        grid=(b, s // tm),
        in_specs=[xspec, const(gpre), const(w_in), const(conv_w), const(conv_b), const(w_out), const(gpost)],
        out_specs=xspec,
        out_shape=jax.ShapeDtypeStruct(x.shape, F32),
        scratch_shapes=[pltpu.VMEM((SUBLANES, 2 * D_FF), F32)],
        compiler_params=_cparams("arbitrary", "arbitrary"),
        name="conv_ffn",
    )(x, gpre, w_in, conv_w, conv_b, w_out, gpost)


def _dilated_qkv_kernel(x_ref, g_ref, w_ref, c_ref, s1_ref, s2_ref, o_ref, xn_ref, ybuf_ref, *, dil):
    sec = pl.program_id(2)
    tm = x_ref.shape[0]

    @pl.when(sec == 0)
    def _():
        xn_ref[...] = (_rms(x_ref[...]) * g_ref[...]).astype(BF16)

    y = _dot(xn_ref[...], w_ref[...])
    on = sec < 2
    c = jnp.where(on, c_ref[...], 1.0)
    s1 = jnp.where(on, s1_ref[...], 0.0)
    s2 = jnp.where(on, s2_ref[...], 0.0)
    scale = jnp.where(sec == 0, A_HEAD_DIM ** -0.5, 1.0)
    y = jnp.concatenate([_rope_tile(y[:, t * LANES:(t + 1) * LANES], c, s1, s2, A_ROT // 2)
                         for t in range(A_WIDTH // LANES)], axis=1) * scale
    if dil == 1:
        o_ref[0] = y.astype(BF16)
    else:
        nt = A_WIDTH // LANES
        for t in range(nt):
            ybuf_ref[t] = y[:, t * LANES:(t + 1) * LANES]
        for r in range(dil):
            o_ref[r] = jnp.concatenate(
                [ybuf_ref[t, pl.ds(r, tm // dil, stride=dil), :] for t in range(nt)], axis=1).astype(BF16)


def _dilated_qkv(x, g_pre, w, tabs, dil, tm=1024):
    b, s, d = x.shape
    tm = min(tm, s)
    nblk = s // tm
    tab = pl.BlockSpec((tm, LANES), lambda bi, i, sec: (bi * nblk + i, 0))
    return pl.pallas_call(
        functools.partial(_dilated_qkv_kernel, dil=dil),
        grid=(b, nblk, 3),
        in_specs=[
            pl.BlockSpec((None, tm, d), lambda bi, i, sec: (bi, i, 0)),
            pl.BlockSpec((1, d), lambda bi, i, sec: (0, 0)),
            pl.BlockSpec((d, A_WIDTH), lambda bi, i, sec: (0, sec)),
            tab, tab, tab,
        ],
        out_specs=pl.BlockSpec((None, dil, tm // dil, A_WIDTH), lambda bi, i, sec: (bi, 0, i, sec)),
        out_shape=jax.ShapeDtypeStruct((b, dil, s // dil, 3 * A_WIDTH), BF16),
        scratch_shapes=[pltpu.VMEM((tm, d), BF16), pltpu.VMEM((A_WIDTH // LANES, tm, LANES), F32)],
        compiler_params=_cparams("parallel", "parallel", "arbitrary"),
        name="dilated_qkv",
    )(x, g_pre.reshape(1, d), w, *tabs)


A_STEP_BLOCKS = 4


def _dilated_attn_kernel(q_ref, kc_ref, kp_ref, vc_ref, vp_ref, o_ref, lse_ref):
    n = pl.program_id(2)
    qi = lax.broadcasted_iota(jnp.int32, (A_BLOCK, A_BLOCK), 0)
    kj = lax.broadcasted_iota(jnp.int32, (A_BLOCK, A_BLOCK), 1)
    band = kj >= qi
    mask_c = kj <= qi
    first = lax.broadcasted_iota(jnp.int32, (A_BLOCK, LANES), 1) < A_HEAD_DIM
    zero = jnp.zeros((), BF16)
    nblk = q_ref.shape[0] // A_BLOCK
    nslab = A_WIDTH // LANES
    tiles = [(j, hp) for j in range(nblk) for hp in range(nslab)]
    rows = lambda j: slice(j * A_BLOCK, (j + 1) * A_BLOCK)
    lanes = lambda hp: slice(hp * LANES, (hp + 1) * LANES)
    grab = lambda ref: jnp.stack([ref[rows(j), lanes(hp)] for j, hp in tiles])
    q, kc, vc = grab(q_ref), grab(kc_ref), grab(vc_ref)
    prev_of = lambda cur_ref, prev_ref: jnp.stack(
        [prev_ref[:, lanes(hp)] if j == 0 else cur_ref[rows(j - 1), lanes(hp)] for j, hp in tiles])
    kp, vp = prev_of(kc_ref, kp_ref), prev_of(vc_ref, vp_ref)
    nt_dims, nn_dims = _BMM_DIMS["nt"], _BMM_DIMS["nn"]
    bdot = lambda a, b, dims: lax.dot_general(a, b, dims, preferred_element_type=F32)
    o_pair = lse_pair = None
    for h in range(2):
        sel = first if h == 0 else jnp.logical_not(first)
        qh = jnp.where(sel, q, zero)
        s_p = jnp.where(band, bdot(qh, kp, nt_dims), NEG_INF)
        s_p = jnp.concatenate([jnp.where(n > 0, s_p[:nslab], NEG_INF), s_p[nslab:]], axis=0)
        s_c = jnp.where(mask_c, bdot(qh, kc, nt_dims), NEG_INF)
        m = jnp.max(jnp.maximum(s_p, s_c), axis=-1, keepdims=True)
        p_p, p_c = jnp.exp(s_p - m), jnp.exp(s_c - m)
        l = jnp.sum(p_p + p_c, axis=-1, keepdims=True)
        o = (bdot(p_p.astype(BF16), jnp.where(sel, vp, zero), nn_dims)
             + bdot(p_c.astype(BF16), jnp.where(sel, vc, zero), nn_dims)) / l
        lse = jnp.broadcast_to(m + jnp.log(l), o.shape)
        o_pair = o if h == 0 else o_pair + o
        lse_pair = lse if h == 0 else jnp.where(first, lse_pair, lse)
    for t, (j, hp) in enumerate(tiles):
        o_ref[rows(j), lanes(hp)] = o_pair[t].astype(o_ref.dtype)
        lse_ref[rows(j), lanes(hp)] = lse_pair[t]


def _dilated_attn(qkv):
    b, dil, ln, _ = qkv.shape
    nblk = min(A_STEP_BLOCKS, ln // A_BLOCK)
    tq = nblk * A_BLOCK

    def cur(section):
        return pl.BlockSpec((None, None, tq, A_WIDTH), lambda bi, r, n: (bi, r, n, section))

    def prev(section):
        return pl.BlockSpec((None, None, A_BLOCK, A_WIDTH),
                            lambda bi, r, n: (bi, r, jnp.maximum(n * nblk - 1, 0), section))

    return pl.pallas_call(
        _dilated_attn_kernel,
        grid=(b, dil, ln // tq),
        in_specs=[cur(0), cur(1), prev(1), cur(2), prev(2)],
        out_specs=[cur(0), cur(0)],
        out_shape=[jax.ShapeDtypeStruct((b, dil, ln, A_WIDTH), BF16),
                   jax.ShapeDtypeStruct((b, dil, ln, A_WIDTH), F32)],
        compiler_params=_cparams("parallel", "parallel", "arbitrary"),
        name="dilated_attn",
    )(qkv, qkv, qkv, qkv, qkv)


def _dilated_combine_kernel(*refs, dils):
    ng = len(dils)
    o_refs, l_refs = refs[:ng], refs[ng:2 * ng]
    w_ref, g_ref, res_ref, out_ref = refs[2 * ng:2 * ng + 4]
    bufs = list(refs[2 * ng + 4:])
    tm = res_ref.shape[0]

    def natural(ref, dil):
        if dil == 1:
            return ref[0].astype(F32)
        buf = bufs.pop()
        nt = A_WIDTH // LANES
        for r in range(dil):
            rows = ref[r].astype(F32)
            for t in range(nt):
                buf[t, pl.ds(r, tm // dil, stride=dil), :] = rows[:, t * LANES:(t + 1) * LANES]
        return jnp.concatenate([buf[t] for t in range(nt)], axis=1)

    os_ = [natural(r, d) for r, d in zip(o_refs, dils)]
    ls = [natural(r, d) for r, d in zip(l_refs, dils)]
    m = functools.reduce(jnp.maximum, ls)
    es = [jnp.exp(l - m) for l in ls]
    o = sum(e * o for e, o in zip(es, os_)) / sum(es)
    y = _dot(o.astype(BF16), w_ref[...])
    out_ref[...] = res_ref[...] + _rms(y) * g_ref[...]


def _dilated_combine(os_, lses, w_o, g, res, tm=512):
    b, s, d = res.shape
    tm = min(tm, s)
    dils = tuple(o.shape[1] for o in os_)
    cls = lambda dil: pl.BlockSpec((None, dil, tm // dil, A_WIDTH), lambda bi, i: (bi, 0, i, 0))
    xspec = pl.BlockSpec((None, tm, d), lambda bi, i: (bi, i, 0))
    n_buf = 2 * sum(dil > 1 for dil in dils)
    return pl.pallas_call(
        functools.partial(_dilated_combine_kernel, dils=dils),
        grid=(b, s // tm),
        in_specs=[cls(dil) for dil in dils] * 2 + [pl.BlockSpec(w_o.shape, lambda bi, i: (0, 0)),
                                                   pl.BlockSpec((1, d), lambda bi, i: (0, 0)), xspec],
        out_specs=xspec,
        out_shape=jax.ShapeDtypeStruct(res.shape, F32),
        scratch_shapes=[pltpu.VMEM((A_WIDTH // LANES, tm, LANES), F32)] * n_buf,
        compiler_params=_cparams("parallel", "parallel"),
        name="dilated_combine",
    )(*os_, *lses, w_o, g.reshape(1, d), res)


def _mixer_dilated(x, tabs_a, g_pre, g_post, w_qkv, w_o):
    ngrp = len(A_GROUPS)
    w = w_qkv.astype(BF16).reshape(w_qkv.shape[0], 3, ngrp, A_WIDTH)
    os_, lses = [], []
    for g, (_, dil) in enumerate(A_GROUPS):
        qkv = _dilated_qkv(x, g_pre, w[:, :, g].reshape(-1, 3 * A_WIDTH), tabs_a, dil)
        o, lse = _dilated_attn(qkv)
        os_.append(o)
        lses.append(lse)
    return _dilated_combine(os_, lses, w_o.astype(BF16), g_post, x)


def _gla_kernel(q_ref, k_ref, v_ref, go_ref, glr_ref, w2_ref, bg_ref, on_ref, o_ref, st_ref):
    i = pl.program_id(2)
    tc = q_ref.shape[0]
    c_len = B_CHUNK

    @pl.when(i == 0)
    def _():
        st_ref[...] = jnp.zeros_like(st_ref)

    nc = tc // c_len
    z = _dot(glr_ref[...].astype(BF16), w2_ref[...]) + bg_ref[...]
    log_a = -_softplus(-z) * (1.0 / B_GATE_NORMALIZER)
    gcum = _chunk_cumsum(log_a, c_len)
    g_end = jnp.concatenate(
        [jnp.broadcast_to(gcum[(c + 1) * c_len - 1:(c + 1) * c_len], (c_len, B_DK)) for c in range(nc)], axis=0)
    q = q_ref[...].astype(F32)
    k = k_ref[...].astype(F32)
    chunks = lambda x: x.reshape(nc, c_len, x.shape[1])
    q_dec = chunks((q * (B_DK ** -0.5) * jnp.exp(gcum)).astype(BF16))
    k_inv = chunks((k * jnp.exp(-gcum)).astype(BF16))
    k_end = chunks(k * jnp.exp(g_end - gcum))
    v = chunks(v_ref[...])
    row = lax.broadcasted_iota(jnp.int32, (c_len, c_len), 0)
    col = lax.broadcasted_iota(jnp.int32, (c_len, c_len), 1)
    bdot = lambda a, b, form: lax.dot_general(a, b, _BMM_DIMS[form], preferred_element_type=F32)
    att = jnp.where(col <= row, bdot(q_dec, k_inv, "nt"), 0.0).astype(BF16)
    o = bdot(att, v, "nn")
    kv = bdot(_bt(v.astype(F32)).astype(BF16), k_end.astype(BF16), "nn")
    st = st_ref[...]
    states = []
    for c in range(nc):
        states.append(st.astype(BF16))
        st = st * jnp.exp(gcum[(c + 1) * c_len - 1:(c + 1) * c_len]) + kv[c]
    st_ref[...] = st
    o = (o + bdot(q_dec, jnp.stack(states), "nt")).reshape(tc, B_DV)
    go = go_ref[...].astype(F32)
    o_ref[...] = (_rms(o) * on_ref[...] * (go * _sigmoid(go))).astype(o_ref.dtype)


def _gla(proj, glr, w_gate2, b_gate, o_norm, tc=512):
    b, s, _ = proj.shape
    tc = min(tc, s)
    nkb = B_KEY_DIM // B_DK
    nvb = 2 * B_KEY_DIM // B_DV
    ngb = nvb + B_VAL_DIM // B_DV
    return pl.pallas_call(
        _gla_kernel,
        grid=(b, B_HEADS, s // tc),
        in_specs=[
            pl.BlockSpec((None, tc, B_DK), lambda bi, h, i: (bi, i, h)),
            pl.BlockSpec((None, tc, B_DK), lambda bi, h, i: (bi, i, nkb + h)),
            pl.BlockSpec((None, tc, B_DV), lambda bi, h, i: (bi, i, nvb + h)),
            pl.BlockSpec((None, tc, B_DV), lambda bi, h, i: (bi, i, ngb + h)),
            pl.BlockSpec((None, tc, LANES), lambda bi, h, i: (bi, i, 0)),
            pl.BlockSpec((LANES, B_DK), lambda bi, h, i: (0, h)),
            pl.BlockSpec((1, B_DK), lambda bi, h, i: (0, h)),
            pl.BlockSpec((1, B_DV), lambda bi, h, i: (0, 0)),
        ],
        out_specs=pl.BlockSpec((None, tc, B_DV), lambda bi, h, i: (bi, i, h)),
        out_shape=jax.ShapeDtypeStruct((b, s, B_VAL_DIM), BF16),
        scratch_shapes=[pltpu.VMEM((B_DV, B_DK), F32)],
        compiler_params=_cparams("parallel", "parallel", "arbitrary"),
        name="gla",
    )(proj, proj, proj, proj, glr, w_gate2, b_gate, o_norm)


def _mixer_gla(x, g_pre, g_post, w_in, w_gate2, b_gate, o_norm, w_o):
    b, s, d = x.shape
    xf = x.reshape(b * s, d)
    main = 2 * B_KEY_DIM + 2 * B_VAL_DIM
    w_main = w_in[:, :main].astype(BF16)
    w_lr = jnp.pad(w_in[:, main:], ((0, 0), (0, LANES - B_GATE_RANK))).astype(BF16)
    proj = _norm_matmul(xf, g_pre, w_main, out_dtype=BF16, tn=main // 4)
    glr = _norm_matmul(xf, g_pre, w_lr, out_dtype=F32, tn=LANES)
    w2 = jnp.pad(w_gate2, ((0, LANES - B_GATE_RANK), (0, 0))).astype(BF16)
    og = _gla(proj.reshape(b, s, main), glr.reshape(b, s, LANES), w2, b_gate.reshape(1, B_KEY_DIM),
              o_norm.reshape(1, B_DV))
    return _matmul_norm_res(og.reshape(b * s, B_VAL_DIM), w_o.astype(BF16), g_post, xf).reshape(b, s, d)


def _mla_proj_kernel(c_ref, qn_ref, kvn_ref, wq_ref, wk_ref, wv_ref, cc_ref, s1_ref, s2_ref,
                     q_ref, k_ref, v_ref):
    c = c_ref[...]
    ckv = c[:, :C_KV_RANK]
    kpe = c[:, C_KV_RANK:C_KV_RANK + C_SLOT]
    cq = c[:, C_KV_RANK + C_SLOT:]
    cc, s1, s2 = cc_ref[...], s1_ref[...], s2_ref[...]
    shift = C_ROPE // 2
    scale = (C_NOPE + C_ROPE) ** -0.5
    q = _dot((_rms(cq) * qn_ref[...]).astype(BF16), wq_ref[...])
    ckvn = (_rms(ckv) * kvn_ref[...]).astype(BF16)
    kn = _dot(ckvn, wk_ref[...])
    kpe = _rope_tile(kpe, cc, s1, s2, shift)
    for h in range(C_HEADS):
        sl = slice(h * C_SLOT, (h + 1) * C_SLOT)
        q_ref[:, sl] = (_rope_tile(q[:, sl], cc, s1, s2, shift) * scale).astype(BF16)
        k_ref[:, sl] = (kn[:, sl] + kpe).astype(BF16)
    v_ref[...] = _dot(ckvn, wv_ref[...]).astype(BF16)


def _mla_proj(c, q_norm, kv_norm, wq, wk, wv, tabs, tm=512):
    t = c.shape[0]
    tm = min(tm, t)
    row = lambda width: pl.BlockSpec((tm, width), lambda i: (i, 0))
    full = lambda a: pl.BlockSpec(a.shape, lambda i: (0, 0))
    qn, kvn = q_norm.reshape(1, C_Q_RANK), kv_norm.reshape(1, C_KV_RANK)
    hw = C_HEADS * C_SLOT
    return pl.pallas_call(
        _mla_proj_kernel,
        grid=(t // tm,),
        in_specs=[row(c.shape[1]), full(qn), full(kvn), full(wq), full(wk), full(wv)] + [row(LANES)] * 3,
        out_specs=[row(hw), row(hw), row(C_HEADS * C_VDIM)],
        out_shape=[jax.ShapeDtypeStruct((t, hw), BF16), jax.ShapeDtypeStruct((t, hw), BF16),
                   jax.ShapeDtypeStruct((t, C_HEADS * C_VDIM), BF16)],
        compiler_params=_cparams("parallel"),
        name="mla_proj",
    )(c, qn, kvn, wq, wk, wv, *tabs)


def _mla_flash_kernel(q_ref, k_ref, v_ref, o_ref, m_ref, l_ref, acc_ref, *, tk):
    i = pl.program_id(2)
    tq = q_ref.shape[0]
    assert tq == tk
    nt = tk // LANES
    m_ref[...] = jnp.full_like(m_ref, NEG_INF)
    l_ref[...] = jnp.zeros_like(l_ref)
    acc_ref[...] = jnp.zeros_like(acc_ref)

    def lane_tiles(x):
        return [x[:, t * LANES:(t + 1) * LANES] for t in range(nt)]

    def step(j, masked):
        start = pl.multiple_of(j * tk, tk)
        v = v_ref[pl.ds(start, tk), :]
        for h in range(2):
            s = _dot_t(q_ref[:, h * C_SLOT:(h + 1) * C_SLOT], k_ref[pl.ds(start, tk), h * C_SLOT:(h + 1) * C_SLOT])
            if masked:
                qi = lax.broadcasted_iota(jnp.int32, (tq, tk), 0)
                kj = lax.broadcasted_iota(jnp.int32, (tq, tk), 1)
                s = jnp.where(kj <= qi, s, NEG_INF)
            tile_max = functools.reduce(jnp.maximum, lane_tiles(s))
            m_old = m_ref[h]
            m_new = jnp.maximum(m_old, jnp.broadcast_to(jnp.max(tile_max, axis=-1, keepdims=True), (tq, LANES)))
            alpha = jnp.exp(m_old - m_new)
            p = jnp.exp(s - jnp.tile(m_new, (1, nt)))
            l_ref[h] = alpha * l_ref[h] + functools.reduce(jnp.add, lane_tiles(p))
            acc_ref[h] = alpha * acc_ref[h] + _dot(p.astype(BF16), v)
            m_ref[h] = m_new

    def body(jj, carry):
        step(2 * jj, False)
        step(2 * jj + 1, False)
        return carry

    lax.fori_loop(0, i // 2, body, 0)

    @pl.when(i % 2 == 1)
    def _():
        step(i - 1, False)

    step(i, True)
    lane = lax.broadcasted_iota(jnp.int32, (tq, 2 * C_VDIM), 1)
    res = [acc_ref[h] / jnp.sum(l_ref[h], axis=-1, keepdims=True) for h in range(2)]
    o_ref[...] = jnp.where(lane < C_VDIM, res[0], res[1]).astype(o_ref.dtype)


def _mla_flash(q, k, v, tq=1024):
    b, s, _ = q.shape
    tq = min(tq, s)
    pairs = C_HEADS // 2
    return pl.pallas_call(
        functools.partial(_mla_flash_kernel, tk=tq),
        grid=(b, pairs, s // tq),
        in_specs=[
            pl.BlockSpec((None, tq, 2 * C_SLOT), lambda bi, p, i: (bi, i, p)),
            pl.BlockSpec((None, s, 2 * C_SLOT), lambda bi, p, i: (bi, 0, p)),
            pl.BlockSpec((None, s, 2 * C_VDIM), lambda bi, p, i: (bi, 0, p)),
        ],
        out_specs=pl.BlockSpec((None, tq, 2 * C_VDIM), lambda bi, p, i: (bi, i, p)),
        out_shape=jax.ShapeDtypeStruct((b, s, C_HEADS * C_VDIM), BF16),
        scratch_shapes=[pltpu.VMEM((2, tq, LANES), F32)] * 3,
        compiler_params=_cparams("parallel", "parallel", "arbitrary"),
        name="mla_flash",
    )(q, k, v)


def _mixer_mla(x, tabs_c, g_pre, g_post, w_in, q_norm, w_uq, kv_norm, w_ukv, w_o):
    b, s, d = x.shape
    xf = x.reshape(b * s, d)
    kpe_w = jnp.pad(w_in[:, C_Q_RANK + C_KV_RANK:], ((0, 0), (C_NOPE, C_SLOT - C_NOPE - C_ROPE)))
    w_c = jnp.concatenate([w_in[:, C_Q_RANK:C_Q_RANK + C_KV_RANK], kpe_w, w_in[:, :C_Q_RANK]], axis=1)
    c = _norm_matmul(xf, g_pre, w_c.astype(BF16), out_dtype=F32, tn=w_c.shape[1])
    pad = C_SLOT - C_NOPE - C_ROPE
    wq = jnp.pad(w_uq.reshape(C_Q_RANK, C_HEADS, C_NOPE + C_ROPE), ((0, 0), (0, 0), (0, pad)))
    wkv = w_ukv.reshape(C_KV_RANK, C_HEADS, C_NOPE + C_VDIM)
    wk = jnp.pad(wkv[:, :, :C_NOPE], ((0, 0), (0, 0), (0, C_SLOT - C_NOPE)))
    wv = wkv[:, :, C_NOPE:]
    q, k, v = _mla_proj(c, q_norm, kv_norm, wq.reshape(C_Q_RANK, -1).astype(BF16),
                        wk.reshape(C_KV_RANK, -1).astype(BF16), wv.reshape(C_KV_RANK, -1).astype(BF16), tabs_c)
    hw = C_HEADS * C_SLOT
    o = _mla_flash(q.reshape(b, s, hw), k.reshape(b, s, hw), v.reshape(b, s, C_HEADS * C_VDIM))
    return _matmul_norm_res(o.reshape(b * s, C_HEADS * C_VDIM), w_o.astype(BF16), g_post, xf).reshape(b, s, d)


def _rwkv_proj_kernel(x_ref, gpre_ref, mix_ref, wr_ref, wk_ref, wv_ref, w1_ref, w2_ref, a1_ref, a2_ref,
                      g1_ref, g2_ref, w0_ref, a0_ref,
                      r_ref, k_ref, v_ref, lw_ref, a_ref, g_ref, hbuf_ref, carry_ref):
    i = pl.program_id(1)
    tm = x_ref.shape[0]
    hn = _rms(x_ref[...]) * gpre_ref[...]
    @pl.when(i == 0)
    def _():
        hbuf_ref[0:SUBLANES, :] = jnp.zeros((SUBLANES, hn.shape[1]), F32)

    @pl.when(i > 0)
    def _():
        hbuf_ref[0:SUBLANES, :] = carry_ref[...]

    hbuf_ref[SUBLANES:SUBLANES + tm, :] = hn
    carry_ref[...] = hn[tm - SUBLANES:, :]
    xx = hbuf_ref[SUBLANES - 1:SUBLANES - 1 + tm, :] - hn
    mix = mix_ref[...]
    mixed = lambda n: (hn + xx * mix[n:n + 1]).astype(BF16)
    r_ref[...] = _dot(mixed(0), wr_ref[...])
    k_ref[...] = _dot(mixed(2), wk_ref[...])
    v_ref[...] = _dot(mixed(3), wv_ref[...])
    wl = _dot(jnp.tanh(_dot(mixed(1), w1_ref[...])).astype(BF16), w2_ref[...])
    w = -_softplus(-(w0_ref[...] + wl)) - 0.5
    lw_ref[...] = -jnp.exp(w)
    al = _dot(_dot(mixed(4), a1_ref[...]).astype(BF16), a2_ref[...])
    a_ref[...] = _sigmoid(a0_ref[...] + al)
    g_ref[...] = _dot(_sigmoid(_dot(mixed(5), g1_ref[...])).astype(BF16), g2_ref[...]).astype(g_ref.dtype)


def _rwkv_proj(x, gpre, mix, w_rkv, w1, w2, a1, a2, g1, g2, w0, a0, tm=512):
    b, s, d = x.shape
    tm = min(tm, s)
    const = lambda a: pl.BlockSpec(a.shape, lambda bi, i: (0,) * a.ndim)
    xspec = pl.BlockSpec((None, tm, d), lambda bi, i: (bi, i, 0))
    bf = lambda a: a.astype(BF16)
    args = [gpre.reshape(1, d), mix, bf(w_rkv[0]), bf(w_rkv[1]), bf(w_rkv[2]), bf(w1), bf(w2), bf(a1), bf(a2),
            bf(g1), bf(g2), w0.reshape(1, d), a0.reshape(1, d)]
    f32_out = jax.ShapeDtypeStruct(x.shape, F32)
    return pl.pallas_call(
        _rwkv_proj_kernel,
        grid=(b, s // tm),
        in_specs=[xspec] + [const(a) for a in args],
        out_specs=[xspec] * 6,
        out_shape=[f32_out] * 5 + [jax.ShapeDtypeStruct(x.shape, BF16)],
        scratch_shapes=[pltpu.VMEM((tm + SUBLANES, d), F32), pltpu.VMEM((SUBLANES, d), F32)],
        compiler_params=_cparams("arbitrary", "arbitrary"),
        name="rwkv_proj",
    )(x, *args)


def _split(x):
    hi = x.astype(BF16)
    return hi, (x - hi.astype(F32)).astype(BF16)


_BMM_DIMS = {
    "nn": (((2,), (1,)), ((0,), (0,))),
    "nt": (((2,), (2,)), ((0,), (0,))),
}


def _bmm(a, b, passes, form="nn"):
    dims = _BMM_DIMS[form]
    dg = functools.partial(lax.dot_general, dimension_numbers=dims, preferred_element_type=F32)
    if passes == 6:
        return dg(a, b, precision=HIGHEST)
    if passes == 1:
        return dg(a.astype(BF16), b.astype(BF16))
    a_hi, a_lo = _split(a)
    b_hi, b_lo = _split(b)
    return dg(a_hi, b_hi) + (dg(a_hi, b_lo) + dg(a_lo, b_hi))


def _bt(x):
    return jnp.swapaxes(x, 1, 2)


RWKV_PASSES = dict(score=1, inv_low=3, inv_high=1, apply=1, state=1, seq=1)


def _rwkv_rec_kernel(r_ref, k_ref, v_ref, lw_ref, a_ref, kk_ref, ka_ref, rk_ref, lnw_ref, lnb_ref,
                     o_ref, st_ref):
    i = pl.program_id(2)
    tc = r_ref.shape[0]
    ln, dh = D_CHUNK, D_HEAD
    nc = tc // ln
    ps = RWKV_PASSES

    @pl.when(i == 0)
    def _():
        st_ref[...] = jnp.zeros_like(st_ref)

    lw = lw_ref[...]
    gc = _chunk_cumsum(lw, ln)
    g_end = jnp.concatenate(
        [jnp.broadcast_to(gc[(c + 1) * ln - 1:(c + 1) * ln], (ln, LANES)) for c in range(nc)], axis=0)

    lane = lax.broadcasted_iota(jnp.int32, (tc, LANES), 1)
    first = lane < dh

    def head_sum(x):
        s0 = jnp.sum(jnp.where(first, x, 0.0), axis=-1, keepdims=True)
        s1 = jnp.sum(jnp.where(first, 0.0, x), axis=-1, keepdims=True)
        return jnp.where(first, s0, s1)

    r, k, v, a = r_ref[...], k_ref[...], v_ref[...], a_ref[...]
    kk = k * kk_ref[...]
    kk = kk / jnp.maximum(jnp.sqrt(head_sum(kk * kk)), 1e-12)
    k_mod = k * (1.0 + (a - 1.0) * ka_ref[...])
    beta = kk * a
    e_neg = jnp.exp(-gc)
    e_end = jnp.exp(g_end - gc)
    a_t = -kk * jnp.exp(gc - lw)
    r_t = r * jnp.exp(gc)

    first_c = lax.broadcasted_iota(jnp.int32, (ln, LANES), 1) < dh

    def pair(x):
        return jnp.stack([jnp.concatenate([jnp.where(first_c, x[c * ln:(c + 1) * ln], 0.0),
                                           jnp.where(first_c, 0.0, x[c * ln:(c + 1) * ln])], axis=0)
                          for c in range(nc)])

    pl_ = 2 * ln
    a_b, r_b, v_b = pair(a_t), pair(r_t), pair(v)
    bend_b, kend_b = pair(beta * e_end), pair(k_mod * e_end)
    sc = _bmm(jnp.concatenate([a_b, r_b], axis=1),
              jnp.concatenate([pair(beta * e_neg), pair(k_mod * e_neg)], axis=1), ps["score"], "nt")
    row = lax.broadcasted_iota(jnp.int32, (pl_, pl_), 0)
    col = lax.broadcasted_iota(jnp.int32, (pl_, pl_), 1)
    incl, strict = (col % ln) <= (row % ln), (col % ln) < (row % ln)
    eye = (row == col).astype(F32)
    n_ab = jnp.where(strict, sc[:, :pl_, :pl_], 0.0)
    a_ak = jnp.where(strict, sc[:, :pl_, pl_:], 0.0)
    a_rb = jnp.where(incl, sc[:, pl_:, :pl_], 0.0)
    a_rk = jnp.where(incl, sc[:, pl_:, pl_:], 0.0)
    n_d = jnp.where((row // D_SUB) == (col // D_SUB), n_ab, 0.0)
    n_o = n_ab - n_d
    p = eye + n_d
    pw = n_d
    n_sq = int(math.log2(D_SUB)) - 1
    for t in range(n_sq):
        passes = ps["inv_low"] if t < n_sq - 1 else ps["inv_high"]
        pw = _bmm(pw, pw, passes)
        p = p + _bmm(p, pw, passes)
    m1 = _bmm(p, n_o, ps["inv_high"])
    m2 = _bmm(m1, m1, ps["inv_high"])
    assert D_CHUNK // D_SUB == 4
    av = _bmm(a_ak, v_b, ps["apply"])
    w12 = _bmm(p, jnp.concatenate([a_b, av], axis=2), ps["apply"])
    w12 = w12 + _bmm(m1, w12, ps["apply"])
    w12 = w12 + _bmm(m2, w12, ps["apply"])
    qy = _bmm(a_rb, w12, ps["apply"])
    q_eff = r_b + qy[:, :, :LANES]
    y0 = qy[:, :, LANES:] + _bmm(a_rk, v_b, ps["apply"])
    w1, w2 = w12[:, :, :LANES], w12[:, :, LANES:]
    decay = jnp.stack([jnp.exp(gc[(c + 1) * ln - 1:(c + 1) * ln]) for c in range(nc)])
    trans = eye * decay + _bmm(_bt(w1), bend_b, ps["state"])
    s_add = _bmm(_bt(jnp.concatenate([w2, v_b], axis=1)), jnp.concatenate([bend_b, kend_b], axis=1), ps["state"])

    st = st_ref[...][None]
    ys = []
    for c in range(nc):
        yp = _bmm(q_eff[c:c + 1], st, ps["seq"], "nt")[0] + y0[c]
        ys.append(yp[:ln] + yp[ln:])
        st = _bmm(st, trans[c:c + 1], ps["seq"]) + s_add[c:c + 1]
    st_ref[...] = st[0]
    y = jnp.concatenate(ys, axis=0)

    mu = head_sum(y) * (1.0 / dh)
    yc = y - mu
    var = head_sum(yc * yc) * (1.0 / dh)
    gn = yc * lax.rsqrt(var + D_GN_EPS) * lnw_ref[...] + lnb_ref[...]
    bonus = head_sum(r * k_mod * rk_ref[...]) * v
    o_ref[...] = gn + bonus


def _rwkv_rec(r, k, v, lw, a, k_k, k_a, r_k, lnx_w, lnx_b, tc=512):
    b, s, d = r.shape
    tc = min(tc, s)
    blk = pl.BlockSpec((None, tc, LANES), lambda bi, p, i: (bi, i, p))
    par = pl.BlockSpec((1, LANES), lambda bi, p, i: (0, p))
    params = [t.reshape(1, d) for t in (k_k, k_a, r_k, lnx_w, lnx_b)]
    return pl.pallas_call(
        _rwkv_rec_kernel,
        grid=(b, d // LANES, s // tc),
        in_specs=[blk] * 5 + [par] * 5,
        out_specs=blk,
        out_shape=jax.ShapeDtypeStruct(r.shape, F32),
        scratch_shapes=[pltpu.VMEM((LANES, LANES), F32)],
        compiler_params=_cparams("parallel", "parallel", "arbitrary"),
        name="rwkv_rec",
    )(r, k, v, lw, a, *params)


def _mixer_rwkv7(x, g_pre, g_post, mix, w_rkv, w0, w1, w2, a0, a1, a2, g1, g2, k_k, k_a, r_k, lnx_w, lnx_b, w_o):
    b, s, d = x.shape
    r, k, v, lw, a, g = _rwkv_proj(x, g_pre, mix, w_rkv, w1, w2, a1, a2, g1, g2, w0, a0)
    y = _rwkv_rec(r, k, v, lw, a, k_k, k_a, r_k, lnx_w, lnx_b)
    flat = lambda t: t.reshape(b * s, d)
    return _matmul_norm_res(flat(y), w_o.astype(BF16), g_post, flat(x), gate=flat(g)).reshape(b, s, d)


def kernel(x, mem, positions, ln_gains, mem_norm, mem_w_kv, mem_w_q, mem_w_o, ffn_w_in, ffn_conv_w, ffn_conv_b, ffn_w_out, a_w_qkv, a_w_o, b_w_in, b_w_gate2, b_gate_bias, b_o_norm, b_w_o, c_w_in, c_q_norm, c_w_uq, c_kv_norm, c_w_ukv, c_w_o, d_mix, d_w_rkv, d_w0, d_w1, d_w2, d_a0, d_a1, d_a2, d_g1, d_g2, d_k_k, d_k_a, d_r_k, d_lnx_w, d_lnx_b, d_w_o):
    b, s, d = x.shape
    depth = ln_gains.shape[0]
    mlen = mem.shape[1]
    mwidth = M_HEADS * M_HEAD_DIM
    mkv = _norm_matmul(mem.reshape(b * mlen, d), mem_norm, mem_w_kv.astype(BF16), out_dtype=BF16,
                       tn=2 * mwidth, tm=mlen)
    mkv = mkv.reshape(b, mlen, 2 * mwidth)
    mem_k, mem_v = mkv[:, :, :mwidth], mkv[:, :, mwidth:]
    tabs = _rope_tables(positions)
    tabs_a, tabs_c = tabs[:3], tabs[3:]
    for i in range(depth):
        m, j = i % N_MIXERS, i // N_MIXERS
        gains = ln_gains[i]
        if m == 0:
            x = _mixer_dilated(x, tabs_a, gains[0], gains[1], a_w_qkv[j], a_w_o[j])
        elif m == 1:
            x = _mixer_gla(x, gains[0], gains[1], b_w_in[j], b_w_gate2[j], b_gate_bias[j], b_o_norm[j], b_w_o[j])
        elif m == 2:
            x = _mixer_mla(x, tabs_c, gains[0], gains[1], c_w_in[j], c_q_norm[j], c_w_uq[j], c_kv_norm[j],
                           c_w_ukv[j], c_w_o[j])
        else:
            x = _mixer_rwkv7(x, gains[0], gains[1], d_mix[j], d_w_rkv[j], d_w0[j], d_w1[j], d_w2[j], d_a0[j],
                             d_a1[j], d_a2[j], d_g1[j], d_g2[j], d_k_k[j], d_k_a[j], d_r_k[j], d_lnx_w[j],
                             d_lnx_b[j], d_w_o[j])
        x = _mem_attn(x, gains[2], mem_w_q[i].astype(BF16), mem_k, mem_v, mem_w_o[i].astype(BF16), gains[3])
        x = _conv_ffn(x, gains[4], ffn_w_in[i], ffn_conv_w[i], ffn_conv_b[i], ffn_w_out[i], gains[5])
    return x
```
